```python
import jax
import jax.numpy as jnp
from jax import lax
import numpy as np

D_MODEL = 2048
BATCH = 8
SEQ = 4096
DEPTH = 4

CHUNK = 64
D_MIX = D_MODEL
N_GROUPS = 4
GROUP_W = D_MIX // N_GROUPS
HEAD_DIM = 64
A_HEADS = GROUP_W // HEAD_DIM
A_HEAD_DIM = HEAD_DIM
A_LORA_W = 64
A_LORA_A = 64
A_LORA_G = 128
A_GN_EPS = A_HEAD_DIM * 1e-5
B_HEADS = GROUP_W // HEAD_DIM
B_HEAD_DIM = HEAD_DIM
IDX_HEADS = 4
IDX_DIM = 64
TOPK_MAX = 256
Q_BLOCK = 128
C_HEADS = GROUP_W // HEAD_DIM
C_HEAD_DIM = HEAD_DIM
LEFT_CHUNKS = 8
REL_CLIP = 256
D_HEADS = GROUP_W // HEAD_DIM
D_KEY_DIM = HEAD_DIM // 2
D_VAL_DIM = HEAD_DIM
RET_GN_EPS = 1e-5
D_FF = 256 * ((8 * D_MODEL // 3 + 255) // 256)
CONV_W = 3
LN_EPS = 1e-5
DEEPNORM_ALPHA = (2 * DEPTH) ** 0.25
DEEPNORM_BETA = (8 * DEPTH) ** -0.25

A_SPLITS = (GROUP_W, GROUP_W, GROUP_W, A_LORA_W, A_LORA_A, A_LORA_G)
B_SPLITS = (GROUP_W, B_HEAD_DIM, B_HEAD_DIM, IDX_HEADS * IDX_DIM, IDX_DIM, IDX_HEADS)
C_SPLITS = (GROUP_W, GROUP_W, GROUP_W)
D_SPLITS = (D_HEADS * D_KEY_DIM, D_HEADS * D_KEY_DIM, GROUP_W, GROUP_W)
A_COLS = sum(A_SPLITS)
B_COLS = sum(B_SPLITS)
C_COLS = sum(C_SPLITS)
D_COLS = sum(D_SPLITS)
N_IN = A_COLS + B_COLS + C_COLS + D_COLS

kernel_name = 'hybrid_stream_encoder'

F32 = jnp.float32


def _split(x, sizes):
    return jnp.split(x, np.cumsum(sizes)[:-1].tolist(), axis=-1)


def _layer_norm(x, g, b):
    xf = x.astype(F32)
    mu = jnp.mean(xf, -1, keepdims=True)
    var = jnp.mean(jnp.square(xf - mu), -1, keepdims=True)
    return ((xf - mu) * lax.rsqrt(var + LN_EPS)).astype(x.dtype) * g + b


def _norm_last(y, eps):
    yf = y.astype(F32)
    mu = jnp.mean(yf, -1, keepdims=True)
    var = jnp.mean(jnp.square(yf - mu), -1, keepdims=True)
    return (yf - mu) * lax.rsqrt(var + eps)


def _shift_right(x):
    return jnp.pad(x, ((0, 0), (1, 0), (0, 0)))[:, :-1]


def _alibi_slopes(n):
    return 2.0 ** (-8.0 * jnp.arange(1, n + 1, dtype=F32) / n)


def _rwkv7_mix(p, mu, w0, w2, a0, a2, g2, k_k, k_a, r_k, ln_g, ln_b):
    bsz, t_len, _ = p.shape
    p = p + (_shift_right(p) - p) * mu
    r, k, v, wl, al, gl = _split(p, A_SPLITS)
    w_log = -jax.nn.softplus(-(w0 + jnp.tanh(wl) @ w2)) - 0.5
    decay = jnp.exp(-jnp.exp(w_log.astype(F32)))
    a = jax.nn.sigmoid(a0 + al @ a2)
    g = jax.nn.sigmoid(gl) @ g2
    hs = lambda z: z.reshape(bsz, t_len, A_HEADS, A_HEAD_DIM)
    kk = hs(k * k_k).astype(F32)
    kk = kk / jnp.maximum(jnp.sqrt(jnp.sum(kk * kk, -1, keepdims=True)), 1e-12)
    k = k * (1.0 + (a - 1.0) * k_a)
    r_h, k_h, v_h, a_h = hs(r), hs(k), hs(v), hs(a)
    seq = tuple(jnp.moveaxis(z.astype(F32), 1, 0) for z in
                (r_h, hs(decay), k_h, v_h, -kk, kk * a_h))

    def step(state, inp):
        r_t, w_t, k_t, v_t, a_t, b_t = inp
        sa = jnp.einsum('bhvk,bhk->bhv', state, a_t)
        state = (state * w_t[:, :, None, :] + sa[..., None] * b_t[:, :, None, :]
                 + v_t[..., None] * k_t[:, :, None, :])
        return state, jnp.einsum('bhvk,bhk->bhv', state, r_t)

    s0 = jnp.zeros((bsz, A_HEADS, A_HEAD_DIM, A_HEAD_DIM), F32)
    _, y = lax.scan(step, s0, seq)
    y = jnp.moveaxis(y, 0, 1)
    y = _norm_last(y, A_GN_EPS).reshape(bsz, t_len, GROUP_W) * ln_g + ln_b
    bonus = jnp.sum(r_h * k_h * r_k, -1, keepdims=True) * v_h
    return (y + bonus.reshape(bsz, t_len, GROUP_W)) * g


def _dsa_mix(q, k, v, qi, ki, wi):
    bsz, t_len, _ = q.shape
    topk = min(TOPK_MAX, t_len // 4)
    nb = t_len // Q_BLOCK
    q = jnp.moveaxis(q.reshape(bsz, nb, Q_BLOCK, B_HEADS, B_HEAD_DIM), 1, 0)
    qi = jnp.moveaxis(qi.reshape(bsz, nb, Q_BLOCK, IDX_HEADS, IDX_DIM), 1, 0)
    wi = jnp.moveaxis(wi.reshape(bsz, nb, Q_BLOCK, IDX_HEADS), 1, 0) * (IDX_HEADS ** -0.5)
    key_pos = jnp.arange(t_len)
    slopes = _alibi_slopes(B_HEADS)
    gather = jax.vmap(lambda table, idx: table[idx])

    def block(args):
        n, qb, qib, wib = args
        t = n * Q_BLOCK + jnp.arange(Q_BLOCK)
        limit = (t // CHUNK + 1) * CHUNK
        logits = jnp.einsum('bqhd,bsd->bqhs', qib, ki) * (IDX_DIM ** -0.5)
        score = jnp.einsum('bqhs,bqh->bqs', jax.nn.relu(logits), wib).astype(F32)
        score = jnp.where(key_pos[None, None, :] < limit[None, :, None], score, -jnp.inf)
        _, idx = lax.top_k(score, topk)
        sel_ok = idx < limit[None, :, None]
        k_sel = gather(k, idx)
        v_sel = gather(v, idx)
        s = jnp.einsum('bqhd,bqkd->bqhk', qb, k_sel).astype(F32) * (B_HEAD_DIM ** -0.5)
        dist = jnp.abs(t[None, :, None] - idx).astype(F32)
        s = s - slopes[None, None, :, None] * dist[:, :, None, :]
        s = jnp.where(sel_ok[:, :, None, :], s, -jnp.inf)
        pr = jax.nn.softmax(s, axis=-1).astype(v.dtype)
        return jnp.einsum('bqhk,bqkd->bqhd', pr, v_sel)

    out = lax.map(block, (jnp.arange(nb), q, qi, wi))
    return jnp.moveaxis(out, 0, 1).reshape(bsz, t_len, GROUP_W)


def _chunk_attn_mix(q, k, v, rel_bias):
    bsz, t_len, _ = q.shape
    nc = t_len // CHUNK
    band = (LEFT_CHUNKS + 1) * CHUNK
    pad = LEFT_CHUNKS * CHUNK
    q = jnp.moveaxis(q.reshape(bsz, nc, CHUNK, C_HEADS, C_HEAD_DIM), 1, 0)
    kp = jnp.pad(k.reshape(bsz, t_len, C_HEADS, C_HEAD_DIM), ((0, 0), (pad, 0), (0, 0), (0, 0)))
    vp = jnp.pad(v.reshape(bsz, t_len, C_HEADS, C_HEAD_DIM), ((0, 0), (pad, 0), (0, 0), (0, 0)))
    i = jnp.arange(CHUNK)[:, None]
    j = jnp.arange(band)[None, :]
    rel = jnp.clip(i - j + pad, -REL_CLIP, REL_CLIP) + REL_CLIP
    bias = rel_bias[:, rel].astype(F32)

    def chunk(args):
        n, qc = args
        kc = lax.dynamic_slice_in_dim(kp, n * CHUNK, band, axis=1)
        vc = lax.dynamic_slice_in_dim(vp, n * CHUNK, band, axis=1)
        s = jnp.einsum('bqhd,bkhd->bhqk', qc, kc).astype(F32) * (C_HEAD_DIM ** -0.5) + bias[None]
        ok = j >= pad - n * CHUNK
        s = jnp.where(ok[None, None], s, -jnp.inf)
        pr = jax.nn.softmax(s, axis=-1).astype(vc.dtype)
        return jnp.einsum('bhqk,bkhd->bqhd', pr, vc)

    out = lax.map(chunk, (jnp.arange(nc), q))
    return jnp.moveaxis(out, 0, 1).reshape(bsz, t_len, GROUP_W)


def _retention_mix(q, k, v, g):
    bsz, t_len, _ = q.shape
    nc = t_len // CHUNK
    q = q.reshape(bsz, nc, CHUNK, D_HEADS, D_KEY_DIM)
    k = k.reshape(bsz, nc, CHUNK, D_HEADS, D_KEY_DIM) * (D_KEY_DIM ** -0.5)
    v = v.reshape(bsz, nc, CHUNK, D_HEADS, D_VAL_DIM)
    log_gamma = jnp.log1p(-(2.0 ** (-5.0 - jnp.arange(D_HEADS, dtype=F32))))
    pos = jnp.arange(CHUNK, dtype=F32)
    diff = pos[:, None] - pos[None, :]
    intra_decay = jnp.where(diff >= 0, jnp.exp(log_gamma[:, None, None] * jnp.maximum(diff, 0.0)), 0.0)
    kv_decay = jnp.exp(log_gamma[:, None] * (CHUNK - 1 - pos)[None, :])
    q_decay = jnp.exp(log_gamma[:, None] * (pos + 1)[None, :])
    chunk_decay = jnp.exp(log_gamma * CHUNK)
    s = jnp.einsum('bnihd,bnjhd->bnhij', q, k).astype(F32) * intra_decay
    intra = jnp.einsum('bnhij,bnjhe->bnihe', s, v.astype(F32))
    kv = jnp.einsum('bnjhd,bnjhe,hj->nbhde', k.astype(F32), v.astype(F32), kv_decay)

    def step(state, kv_n):
        return state * chunk_decay[None, :, None, None] + kv_n, state

    r0 = jnp.zeros((bsz, D_HEADS, D_KEY_DIM, D_VAL_DIM), F32)
    _, r_prev = lax.scan(step, r0, kv)
    inter = jnp.einsum('bnihd,hi,nbhde->bnihe', q.astype(F32), q_decay, r_prev)
    y = _norm_last(intra + inter, RET_GN_EPS).reshape(bsz, t_len, GROUP_W)
    return y * jax.nn.silu(g)


def _causal_dwconv(x, w, b):
    out = lax.conv_general_dilated(
        x, w[:, None, :].astype(x.dtype), window_strides=(1,),
        padding=((CONV_W - 1, 0),), dimension_numbers=('NWC', 'WIO', 'NWC'),
        feature_group_count=x.shape[-1])
    return out + b


def setup_inputs(seed: int = 0) -> dict:
    key = jax.random.key(seed)
    ks = iter(jax.random.split(key, 32))
    nrm = lambda shape, scale: jax.random.normal(next(ks), shape, F32) * scale
    L = DEPTH
    return {
        'x': nrm((BATCH, SEQ, D_MODEL), 1.0),
        'c': nrm((BATCH, D_MODEL), 1.0),
        'w_mod': nrm((L, D_MODEL, 6 * D_MODEL), 0.2 * D_MODEL ** -0.5),
        'b_mod': nrm((L, 6 * D_MODEL), 0.01),
        'w_in': nrm((L, D_MODEL, N_IN), D_MODEL ** -0.5),
        'rwkv_mu': jax.random.uniform(next(ks), (L, A_COLS), F32),
        'rwkv_w0': jax.random.uniform(next(ks), (L, GROUP_W), F32, -6.0, 1.0),
        'rwkv_w2': nrm((L, A_LORA_W, GROUP_W), 0.1 * A_LORA_W ** -0.5),
        'rwkv_a0': nrm((L, GROUP_W), 0.1),
        'rwkv_a2': nrm((L, A_LORA_A, GROUP_W), 0.1 * A_LORA_A ** -0.5),
        'rwkv_g2': nrm((L, A_LORA_G, GROUP_W), A_LORA_G ** -0.5),
        'rwkv_kk': 0.85 + nrm((L, GROUP_W), 0.05),
        'rwkv_ka': 1.0 + nrm((L, GROUP_W), 0.05),
        'rwkv_rk': nrm((L, A_HEADS, A_HEAD_DIM), 0.1),
        'rwkv_ln_g': 1.0 + nrm((L, GROUP_W), 0.02),
        'rwkv_ln_b': nrm((L, GROUP_W), 0.02),
        'chunk_rel_bias': nrm((L, C_HEADS, 2 * REL_CLIP + 1), 0.1),
        'w_out': nrm((L, D_MIX, D_MODEL), DEEPNORM_BETA * D_MIX ** -0.5),
        'ln1_g': 1.0 + nrm((L, D_MODEL), 0.02),
        'ln1_b': nrm((L, D_MODEL), 0.02),
        'w_up': nrm((L, D_MODEL, 2 * D_FF), D_MODEL ** -0.5),
        'conv_w': nrm((L, CONV_W, D_FF), CONV_W ** -0.5),
        'conv_b': nrm((L, D_FF), 0.02),
        'w_down': nrm((L, D_FF, D_MODEL), DEEPNORM_BETA * D_FF ** -0.5),
        'ln2_g': 1.0 + nrm((L, D_MODEL), 0.02),
        'ln2_b': nrm((L, D_MODEL), 0.02),
    }


def reference(x, c, w_mod, b_mod, w_in, rwkv_mu, rwkv_w0, rwkv_w2, rwkv_a0, rwkv_a2,
              rwkv_g2, rwkv_kk, rwkv_ka, rwkv_rk, rwkv_ln_g, rwkv_ln_b, chunk_rel_bias,
              w_out, ln1_g, ln1_b, w_up, conv_w, conv_b, w_down, ln2_g, ln2_b):
    c_act = jax.nn.silu(c)
    for l in range(DEPTH):
        mod = c_act @ w_mod[l] + b_mod[l]
        sh1, sc1, gt1, sh2, sc2, gt2 = [m[:, None, :] for m in jnp.split(mod, 6, axis=-1)]

        h = x * (1.0 + sc1) + sh1
        proj = h @ w_in[l]
        p_a, p_b, p_c, p_d = _split(proj, (A_COLS, B_COLS, C_COLS, D_COLS))
        y_a = _rwkv7_mix(p_a, rwkv_mu[l], rwkv_w0[l], rwkv_w2[l], rwkv_a0[l], rwkv_a2[l],
                         rwkv_g2[l], rwkv_kk[l], rwkv_ka[l], rwkv_rk[l], rwkv_ln_g[l], rwkv_ln_b[l])
        qb, kb, vb, qib, kib, wib = _split(p_b, B_SPLITS)
        y_b = _dsa_mix(qb, kb, vb, qib, kib, wib)
        qc, kc, vc = _split(p_c, C_SPLITS)
        y_c = _chunk_attn_mix(qc, kc, vc, chunk_rel_bias[l])
        qd, kd, vd, gd = _split(p_d, D_SPLITS)
        y_d = _retention_mix(qd, kd, vd, gd)
        mix = jnp.concatenate([y_a, y_b, y_c, y_d], axis=-1) @ w_out[l]
        x = _layer_norm(DEEPNORM_ALPHA * x + (1.0 + gt1) * mix, ln1_g[l], ln1_b[l])

        h = x * (1.0 + sc2) + sh2
        gate, up = _split(h @ w_up[l], (D_FF, D_FF))
        f = jax.nn.silu(_causal_dwconv(gate, conv_w[l], conv_b[l])) * up
        x = _layer_norm(DEEPNORM_ALPHA * x + (1.0 + gt2) * (f @ w_down[l]), ln2_g[l], ln2_b[l])
    return x
```

```python
import functools

import numpy as np
import jax
import jax.numpy as jnp
from jax import lax
from jax.experimental import pallas as pl
from jax.experimental.pallas import tpu as pltpu

F32 = jnp.float32
BF16 = jnp.bfloat16
I32 = jnp.int32
HIGHEST = lax.Precision.HIGHEST

D_MODEL = 2048
DEPTH = 4
CHUNK = 64
GROUP_W = 512
HEAD_DIM = 64
N_HEADS = 8
A_LORA_W, A_LORA_A, A_LORA_G = 64, 64, 128
A_GN_EPS = HEAD_DIM * 1e-5
IDX_HEADS, IDX_DIM = 4, 64
TOPK_MAX = 256
Q_BLOCK = 128
LEFT_CHUNKS = 8
REL_CLIP = 256
D_KEY_DIM = 32
RET_GN_EPS = 1e-5
D_FF = 5632
CONV_W = 3
LN_EPS = 1e-5
DEEPNORM_ALPHA = (2 * DEPTH) ** 0.25

A_COLS = 3 * GROUP_W + A_LORA_W + A_LORA_A + A_LORA_G
B_COLS = GROUP_W + 2 * HEAD_DIM + IDX_HEADS * IDX_DIM + IDX_DIM + IDX_HEADS
C_COLS = 3 * GROUP_W
D_COLS = 2 * N_HEADS * D_KEY_DIM + 2 * GROUP_W

LANES = 128
SUBLANES = 8
VMEM_LIMIT = 56 * 1024 * 1024

P_COLS = 6144
PC_Q, PC_K, PC_V = 0, 512, 1024
PD_Q, PD_K, PD_V, PD_G = 1536, 1792, 2048, 2560
PB_Q, PB_QI, PB_KV = 3072, 3584, 3840
PA = 4096
INT_MIN = -2 ** 31


def _cparams(sem):
    return pltpu.CompilerParams(dimension_semantics=sem, vmem_limit_bytes=VMEM_LIMIT)


def _sigmoid(x):
    return 1.0 / (1.0 + jnp.exp(-x))


def _silu(x):
    return x * _sigmoid(x)


def _dot(a, b, **kw):
    return jnp.dot(a, b, preferred_element_type=F32, **kw)


def _dot_nt(a, b, **kw):
    return lax.dot_general(a, b, (((1,), (1,)), ((), ())), preferred_element_type=F32, **kw)


def _mod_kernel(c_ref, w_ref, b_ref, o_ref):
    ca = _silu(c_ref[...])
    o_ref[0] = _dot(ca, w_ref[0], precision=HIGHEST) + b_ref[0]


def _modulation(c, w_mod, b_mod):
    depth, d, n6 = w_mod.shape
    bsz = c.shape[0]
    tn = 1024
    return pl.pallas_call(
        _mod_kernel,
        out_shape=jax.ShapeDtypeStruct((depth, bsz, n6), F32),
        grid=(depth, n6 // tn),
        in_specs=[
            pl.BlockSpec((bsz, d), lambda l, j: (0, 0)),
            pl.BlockSpec((1, d, tn), lambda l, j: (l, 0, j)),
            pl.BlockSpec((1, 1, tn), lambda l, j: (l, 0, j)),
        ],
        out_specs=pl.BlockSpec((1, bsz, tn), lambda l, j: (l, 0, j)),
        compiler_params=_cparams(("parallel", "parallel")),
        name="adaln_mod",
    )(c, w_mod, b_mod.reshape(depth, 1, n6))


def _inproj_kernel(x_ref, sc_ref, sh_ref, w_ref, o_ref, h_scr):
    @pl.when(pl.program_id(2) == 0)
    def _():
        h_scr[...] = (x_ref[0] * (1.0 + sc_ref[0]) + sh_ref[0]).astype(BF16)

    o_ref[0] = _dot(h_scr[...], w_ref[...])


def _in_projection(x, sc, sh, w):
    bsz, t_len, d = x.shape
    n = w.shape[1]
    tm = min(512, t_len)
    tn = 512
    return pl.pallas_call(
        _inproj_kernel,
        out_shape=jax.ShapeDtypeStruct((bsz, t_len, n), F32),
        grid=(bsz, t_len // tm, n // tn),
        in_specs=[
            pl.BlockSpec((1, tm, d), lambda b, i, j: (b, i, 0)),
            pl.BlockSpec((1, 1, d), lambda b, i, j: (b, 0, 0)),
            pl.BlockSpec((1, 1, d), lambda b, i, j: (b, 0, 0)),
            pl.BlockSpec((d, tn), lambda b, i, j: (0, j)),
        ],
        out_specs=pl.BlockSpec((1, tm, tn), lambda b, i, j: (b, i, j)),
        scratch_shapes=[pltpu.VMEM((tm, d), BF16)],
        compiler_params=_cparams(("parallel", "parallel", "arbitrary")),
        name="in_proj",
    )(x, sc, sh, w)


def _rwkv_prep_kernel(p_ref, prev_ref, mu_ref, w0_ref, w2_ref, a0_ref, a2_ref, g2_ref,
                      kk_ref, ka_ref, seg_ref,
                      r_out, w_out, k_out, v_out, a_out, b_out, g_out):
    i = pl.program_id(1)
    p = p_ref[0][:, :A_COLS]
    tb = p.shape[0]
    last_prev = prev_ref[0][SUBLANES - 1:SUBLANES, :A_COLS]
    last_prev = jnp.where(i == 0, 0.0, last_prev)
    rows = lax.broadcasted_iota(I32, p.shape, 0)
    shifted = jnp.where(rows == 0, last_prev, pltpu.roll(p, 1, 0))
    xs = p + (shifted - p) * mu_ref[...]
    r = xs[:, 0:512]
    k = xs[:, 512:1024]
    v = xs[:, 1024:1536]
    wl = xs[:, 1536:1600]
    al = xs[:, 1600:1664]
    gl = xs[:, 1664:1792]
    z = w0_ref[...] + _dot(jnp.tanh(wl), w2_ref[...], precision=HIGHEST)
    nz = -z
    softplus = jnp.maximum(nz, 0.0) + jnp.log1p(jnp.exp(-jnp.abs(nz)))
    w_log = -softplus - 0.5
    decay = jnp.exp(-jnp.exp(w_log))
    a = _sigmoid(a0_ref[...] + _dot(al, a2_ref[...], precision=HIGHEST))
    g = _dot(_sigmoid(gl), g2_ref[...], precision=HIGHEST)
    kk = k * kk_ref[...]
    ssq = _dot(kk * kk, seg_ref[...], precision=HIGHEST)
    kk = kk / jnp.maximum(jnp.sqrt(ssq), 1e-12)
    k2 = k * (1.0 + (a - 1.0) * ka_ref[...])
    g_out[0] = g
    na = -kk
    bb = kk * a
    for h in range(N_HEADS):
        sl = slice(h * HEAD_DIM, (h + 1) * HEAD_DIM)
        r_out[0, h] = r[:, sl]
        w_out[0, h] = decay[:, sl]
        k_out[0, h] = k2[:, sl]
        v_out[0, h] = v[:, sl]
        a_out[0, h] = na[:, sl]
        b_out[0, h] = bb[:, sl]


def _rwkv_prep(proj, mu, w0, w2, a0, a2, g2, k_k, k_a):
    bsz, t_len, _ = proj.shape
    tb = min(256, t_len)
    a_blk = PA // 2048
    seg = (np.arange(GROUP_W)[:, None] // HEAD_DIM == np.arange(GROUP_W)[None, :] // HEAD_DIM)
    seg = jnp.asarray(seg, F32)
    row = lambda z: z.reshape(1, -1)
    full = lambda shp: pl.BlockSpec(shp, lambda b, i: (0,) * len(shp))
    head_out = jax.ShapeDtypeStruct((bsz, N_HEADS, t_len, HEAD_DIM), F32)
    head_spec = pl.BlockSpec((1, N_HEADS, tb, HEAD_DIM), lambda b, i: (b, 0, i, 0))
    return pl.pallas_call(
        _rwkv_prep_kernel,
        out_shape=[head_out] * 6 + [jax.ShapeDtypeStruct((bsz, t_len, GROUP_W), F32)],
        grid=(bsz, t_len // tb),
        in_specs=[
            pl.BlockSpec((1, tb, 2048), lambda b, i: (b, i, a_blk)),
            pl.BlockSpec((1, SUBLANES, 2048),
                         lambda b, i: (b, jnp.maximum(i * (tb // SUBLANES) - 1, 0), a_blk)),
            full((1, A_COLS)), full((1, GROUP_W)), full((A_LORA_W, GROUP_W)),
            full((1, GROUP_W)), full((A_LORA_A, GROUP_W)), full((A_LORA_G, GROUP_W)),
            full((1, GROUP_W)), full((1, GROUP_W)), full((GROUP_W, GROUP_W)),
        ],
        out_specs=[head_spec] * 6 + [pl.BlockSpec((1, tb, GROUP_W), lambda b, i: (b, i, 0))],
        compiler_params=_cparams(("parallel", "parallel")),
        name="rwkv_prep",
    )(proj, proj, row(mu), row(w0), w2, row(a0), a2, g2, row(k_k), row(k_a), seg)


SCAN_UNROLL = SUBLANES


def _rwkv_scan_kernel(r_ref, w_ref, k_ref, v_ref, a_ref, b_ref, y_ref, s_scr):
    @pl.when(pl.program_id(1) == 0)
    def _():
        s_scr[...] = jnp.zeros_like(s_scr)

    bb = r_ref.shape[0]
    tb = r_ref.shape[2]
    chains = [(bi, h) for bi in range(bb) for h in range(N_HEADS)]
    eye = (lax.broadcasted_iota(I32, (HEAD_DIM, HEAD_DIM), 0)
           == lax.broadcasted_iota(I32, (HEAD_DIM, HEAD_DIM), 1))

    def sub_block(j, carry):
        t0 = pl.multiple_of(j * SCAN_UNROLL, SCAN_UNROLL)
        rows = pl.ds(t0, SCAN_UNROLL)
        tiles = [tuple(ref[bi, h, rows, :] for ref in (r_ref, w_ref, k_ref, v_ref, a_ref, b_ref))
                 for (bi, h) in chains]
        for u in range(SCAN_UNROLL):
            for ci, (bi, h) in enumerate(chains):
                r_t, w_t, k_t, v_t, a_t, b_t = (z[u:u + 1, :] for z in tiles[ci])
                s = s_scr[ci]
                sa = jnp.sum(s * a_t, axis=-1, keepdims=True)
                v_col = jnp.sum(jnp.where(eye, v_t, 0.0), axis=-1, keepdims=True)
                s = s * w_t + sa * b_t + v_col * k_t
                s_scr[ci] = s
                y_col = jnp.sum(s * r_t, axis=-1, keepdims=True)
                y_row = jnp.sum(jnp.where(eye, y_col, 0.0), axis=0, keepdims=True)
                y_ref[bi, h, pl.ds(t0 + u, 1), :] = y_row
        return carry

    lax.fori_loop(0, tb // SCAN_UNROLL, sub_block, 0)


def _rwkv_scan(r, w, k, v, a, b):
    bsz, nh, t_len, hd = r.shape
    bb = 1
    tb = min(256, t_len)
    spec = pl.BlockSpec((bb, nh, tb, hd), lambda bi, i: (bi, 0, i, 0))
    return pl.pallas_call(
        _rwkv_scan_kernel,
        out_shape=jax.ShapeDtypeStruct(r.shape, F32),
        grid=(bsz // bb, t_len // tb),
        in_specs=[spec] * 6,
        out_specs=spec,
        scratch_shapes=[pltpu.VMEM((bb * nh, hd, hd), F32)],
        compiler_params=_cparams(("parallel", "arbitrary")),
        name="rwkv_scan",
    )(r, w, k, v, a, b)


def _rwkv_post_kernel(y_ref, r_ref, k_ref, v_ref, g_ref, lng_ref, lnb_ref, rk_ref, o_ref):
    outs = []
    for h in range(N_HEADS):
        y = y_ref[0, h]
        mu = jnp.mean(y, axis=-1, keepdims=True)
        d = y - mu
        var = jnp.mean(d * d, axis=-1, keepdims=True)
        yn = d * lax.rsqrt(var + A_GN_EPS) * lng_ref[h] + lnb_ref[h]
        v = v_ref[0, h]
        bonus = jnp.sum(r_ref[0, h] * k_ref[0, h] * rk_ref[h], axis=-1, keepdims=True) * v
        outs.append(yn + bonus)
    o_ref[0] = jnp.concatenate(outs, axis=-1) * g_ref[0]


def _rwkv_post(y, r, k, v, g, ln_g, ln_b, r_k):
    bsz, nh, t_len, hd = y.shape
    tb = min(256, t_len)
    head_spec = pl.BlockSpec((1, nh, tb, hd), lambda b, i: (b, 0, i, 0))
    par_spec = pl.BlockSpec((nh, 1, hd), lambda b, i: (0, 0, 0))
    flat_spec = pl.BlockSpec((1, tb, GROUP_W), lambda b, i: (b, i, 0))
    return pl.pallas_call(
        _rwkv_post_kernel,
        out_shape=jax.ShapeDtypeStruct((bsz, t_len, GROUP_W), F32),
        grid=(bsz, t_len // tb),
        in_specs=[head_spec] * 4 + [flat_spec, par_spec, par_spec, par_spec],
        out_specs=flat_spec,
        compiler_params=_cparams(("parallel", "parallel")),
        name="rwkv_post",
    )(y, r, k, v, g, ln_g.reshape(nh, 1, hd), ln_b.reshape(nh, 1, hd), r_k.reshape(nh, 1, hd))


DSA_KC = 512


def _dsa_kernel(q_ref, qi_ref, kv_ref, o_ref, key_scr, bias_scr, s_scr, *, topk):
    n = pl.program_id(1)
    t_len = kv_ref.shape[1]
    nq = Q_BLOCK
    kc = min(DSA_KC, t_len)
    n_chunks = (n * nq + nq + kc - 1) // kc
    t_row = n * nq + lax.broadcasted_iota(I32, (nq, 1), 0)
    limit = (t_row // CHUNK + 1) * CHUNK

    qi = qi_ref[0]
    wi = kv_ref[0, pl.ds(pl.multiple_of(n * nq, nq), nq), 3 * HEAD_DIM:3 * HEAD_DIM + IDX_HEADS]
    wi = wi * (IDX_HEADS ** -0.5)

    def score_chunk(c, carry):
        k0 = pl.multiple_of(c * kc, kc)
        ki = kv_ref[0, pl.ds(k0, kc), 2 * HEAD_DIM:3 * HEAD_DIM]
        score = jnp.zeros((nq, kc), F32)
        for h in range(IDX_HEADS):
            lg = _dot_nt(qi[:, h * IDX_DIM:(h + 1) * IDX_DIM], ki, precision=HIGHEST)
            lg = lg * (IDX_DIM ** -0.5)
            score = score + jnp.maximum(lg, 0.0) * wi[:, h:h + 1]
        score = jnp.where(score == 0.0, 0.0, score)
        bits = pltpu.bitcast(score, I32)
        okey = jnp.where(bits >= 0, bits, bits ^ jnp.int32(0x7FFFFFFF))
        pos = k0 + lax.broadcasted_iota(I32, (nq, kc), 1)
        key_scr[c] = jnp.where(pos < limit, okey, jnp.int32(INT_MIN))
        return carry

    lax.fori_loop(0, n_chunks, score_chunk, 0)

    def count_ge(cand):
        def body(c, acc):
            return acc + jnp.sum((key_scr[c] >= cand).astype(F32), axis=-1, keepdims=True)
        return lax.fori_loop(0, n_chunks, body, jnp.zeros((nq, 1), F32))

    kf = jnp.float32(topk)
    m0 = jnp.where(count_ge(jnp.zeros((nq, 1), I32)) >= kf, jnp.int32(0), jnp.int32(INT_MIN))

    def bit_step(i, m):
        cand = m | (jnp.int32(1) << (jnp.int32(30) - i))
        return jnp.where(count_ge(cand) >= kf, cand, m)

    thr = lax.fori_loop(0, 31, bit_step, m0 * jnp.ones((nq, 1), I32))
    cnt_gt = count_ge(thr + 1)
    need = kf - cnt_gt

    tri = (lax.broadcasted_iota(I32, (LANES, LANES), 0)
           <= lax.broadcasted_iota(I32, (LANES, LANES), 1)).astype(BF16)

    def select_chunk(c, running):
        keys = key_scr[c]
        parts = []
        for s in range(kc // LANES):
            kt = keys[:, s * LANES:(s + 1) * LANES]
            tie = kt == thr
            prefix = _dot(jnp.where(tie, 1.0, 0.0).astype(BF16), tri)
            rank = running + prefix
            sel = (kt > thr) | (tie & (rank <= need) & (kt != jnp.int32(INT_MIN)))
            parts.append(jnp.where(sel, 0.0, -jnp.inf))
            running = running + prefix[:, LANES - 1:LANES]
        bias_scr[c] = jnp.concatenate(parts, axis=-1)
        return running

    lax.fori_loop(0, n_chunks, select_chunk, jnp.zeros((nq, 1), F32))

    outs = []
    for h in range(N_HEADS):
        slope = 2.0 ** (-8.0 * (h + 1) / N_HEADS)
        qh = q_ref[0][:, h * HEAD_DIM:(h + 1) * HEAD_DIM].astype(BF16)

        def logits_chunk(c, mx):
            k0 = pl.multiple_of(c * kc, kc)
            kk = kv_ref[0, pl.ds(k0, kc), 0:HEAD_DIM].astype(BF16)
            s = _dot_nt(qh, kk) * (HEAD_DIM ** -0.5)
            pos = k0 + lax.broadcasted_iota(I32, (nq, kc), 1)
            dist = jnp.abs(t_row - pos).astype(F32)
            s = s - slope * dist + bias_scr[c]
            s_scr[c] = s
            return jnp.maximum(mx, jnp.max(s, axis=-1, keepdims=True))

        mx = lax.fori_loop(0, n_chunks, logits_chunk, jnp.full((nq, 1), -jnp.inf, F32))

        def pv_chunk(c, carry):
            l, acc = carry
            k0 = pl.multiple_of(c * kc, kc)
            vv = kv_ref[0, pl.ds(k0, kc), HEAD_DIM:2 * HEAD_DIM].astype(BF16)
            p = jnp.exp(s_scr[c] - mx)
            l = l + jnp.sum(p, axis=-1, keepdims=True)
            acc = acc + _dot(p.astype(BF16), vv)
            return l, acc

        l, acc = lax.fori_loop(0, n_chunks, pv_chunk,
                               (jnp.zeros((nq, 1), F32), jnp.zeros((nq, HEAD_DIM), F32)))
        outs.append(acc / l)
    o_ref[0] = jnp.concatenate(outs, axis=-1)


def _dsa_mix(proj):
    bsz, t_len, _ = proj.shape
    topk = min(TOPK_MAX, t_len // 4)
    kc = min(DSA_KC, t_len)
    nck = t_len // kc
    return pl.pallas_call(
        functools.partial(_dsa_kernel, topk=topk),
        out_shape=jax.ShapeDtypeStruct((bsz, t_len, GROUP_W), F32),
        grid=(bsz, t_len // Q_BLOCK),
        in_specs=[
            pl.BlockSpec((1, Q_BLOCK, 512), lambda b, n: (b, n, PB_Q // 512)),
            pl.BlockSpec((1, Q_BLOCK, 256), lambda b, n: (b, n, PB_QI // 256)),
            pl.BlockSpec((1, t_len, 256), lambda b, n: (b, 0, PB_KV // 256)),
        ],
        out_specs=pl.BlockSpec((1, Q_BLOCK, GROUP_W), lambda b, n: (b, n, 0)),
        scratch_shapes=[pltpu.VMEM((nck, Q_BLOCK, kc), I32),
                        pltpu.VMEM((nck, Q_BLOCK, kc), F32),
                        pltpu.VMEM((nck, Q_BLOCK, kc), F32)],
        compiler_params=_cparams(("parallel", "arbitrary")),
        name="dsa_mix",
    )(proj, proj, proj)


BAND_TQ = LEFT_CHUNKS * CHUNK
BAND_W = (LEFT_CHUNKS + 1) * CHUNK


def _band_kernel(q_ref, kp_ref, kc_ref, vp_ref, vc_ref, bias_ref, o_ref):
    i = pl.program_id(1)
    tq = q_ref.shape[1]
    kwin = jnp.concatenate([kp_ref[0], kc_ref[0]], axis=0).astype(BF16)
    vwin = jnp.concatenate([vp_ref[0], vc_ref[0]], axis=0).astype(BF16)
    col = lax.broadcasted_iota(I32, (CHUNK, BAND_W), 1)
    for c in range(tq // CHUNK):
        r0 = c * CHUNK
        qc = q_ref[0, r0:r0 + CHUNK, :].astype(BF16)
        kc_ = kwin[r0:r0 + BAND_W]
        vc_ = vwin[r0:r0 + BAND_W]
        valid = jnp.logical_or(i > 0, col >= tq - r0)
        outs = []
        for h in range(N_HEADS):
            sl = slice(h * HEAD_DIM, (h + 1) * HEAD_DIM)
            s = _dot_nt(qc[:, sl], kc_[:, sl]) * (HEAD_DIM ** -0.5) + bias_ref[h]
            s = jnp.where(valid, s, -jnp.inf)
            mx = jnp.max(s, axis=-1, keepdims=True)
            p = jnp.exp(s - mx)
            l = jnp.sum(p, axis=-1, keepdims=True)
            outs.append(_dot(p.astype(BF16), vc_[:, sl]) / l)
        o_ref[0, r0:r0 + CHUNK, :] = jnp.concatenate(outs, axis=-1)


def _band_mix(proj, rel_bias):
    bsz, t_len, _ = proj.shape
    tq = BAND_TQ
    i_ = np.arange(CHUNK)[:, None]
    j_ = np.arange(BAND_W)[None, :]
    rel = np.clip(i_ - j_ + LEFT_CHUNKS * CHUNK, -REL_CLIP, REL_CLIP) + REL_CLIP
    bias = rel_bias[:, rel].astype(F32)
    cur = lambda blk: pl.BlockSpec((1, tq, 512), lambda b, i: (b, i, blk))
    prev = lambda blk: pl.BlockSpec((1, tq, 512), lambda b, i: (b, jnp.maximum(i - 1, 0), blk))
    return pl.pallas_call(
        _band_kernel,
        out_shape=jax.ShapeDtypeStruct((bsz, t_len, GROUP_W), F32),
        grid=(bsz, t_len // tq),
        in_specs=[cur(PC_Q // 512), prev(PC_K // 512), cur(PC_K // 512),
                  prev(PC_V // 512), cur(PC_V // 512),
                  pl.BlockSpec((N_HEADS, CHUNK, BAND_W), lambda b, i: (0, 0, 0))],
        out_specs=pl.BlockSpec((1, tq, GROUP_W), lambda b, i: (b, i, 0)),
        compiler_params=_cparams(("parallel", "parallel")),
        name="band_attn",
    )(proj, proj, proj, proj, proj, bias)


RET_TB = 512


def _ret_kernel(q_ref, k_ref, v_ref, g_ref, intra_ref, kvd_ref, qd_ref, cd_ref, o_ref, st_scr):
    @pl.when(pl.program_id(1) == 0)
    def _():
        st_scr[...] = jnp.zeros_like(st_scr)

    tb = q_ref.shape[1]
    eye = (lax.broadcasted_iota(I32, (D_KEY_DIM, D_KEY_DIM), 0)
           == lax.broadcasted_iota(I32, (D_KEY_DIM, D_KEY_DIM), 1)).astype(BF16)
    for c in range(tb // CHUNK):
        r0 = c * CHUNK
        q = q_ref[0, r0:r0 + CHUNK, :]
        k = k_ref[0, r0:r0 + CHUNK, :] * (D_KEY_DIM ** -0.5)
        v = v_ref[0, r0:r0 + CHUNK, :]
        g = g_ref[0, r0:r0 + CHUNK, :]
        outs = []
        for h in range(N_HEADS):
            ks = slice(h * D_KEY_DIM, (h + 1) * D_KEY_DIM)
            vs = slice(h * HEAD_DIM, (h + 1) * HEAD_DIM)
            qh, kh, vh = q[:, ks], k[:, ks], v[:, vs]
            vb = vh.astype(BF16)
            s = _dot_nt(qh.astype(BF16), kh.astype(BF16)) * intra_ref[h]
            intra = _dot(s.astype(BF16), vb)
            state = st_scr[h]
            inter = _dot((qh * qd_ref[h]).astype(BF16), state.astype(BF16))
            kd_t = _dot_nt(eye, (kh * kvd_ref[h]).astype(BF16)).astype(BF16)
            st_scr[h] = state * cd_ref[h] + _dot(kd_t, vb)
            y = intra + inter
            mu = jnp.mean(y, axis=-1, keepdims=True)
            d = y - mu
            var = jnp.mean(d * d, axis=-1, keepdims=True)
            outs.append(d * lax.rsqrt(var + RET_GN_EPS))
        o_ref[0, r0:r0 + CHUNK, :] = jnp.concatenate(outs, axis=-1) * _silu(g)


def _ret_mix(proj):
    bsz, t_len, _ = proj.shape
    tb = min(RET_TB, t_len)
    log_gamma = jnp.log1p(-(2.0 ** (-5.0 - jnp.arange(N_HEADS, dtype=F32))))
    pos = jnp.arange(CHUNK, dtype=F32)
    diff = pos[:, None] - pos[None, :]
    intra_decay = jnp.where(diff >= 0, jnp.exp(log_gamma[:, None, None] * jnp.maximum(diff, 0.0)), 0.0)
    kv_decay = jnp.exp(log_gamma[:, None] * (CHUNK - 1 - pos)[None, :])[:, :, None]
    q_decay = jnp.exp(log_gamma[:, None] * (pos + 1)[None, :])[:, :, None]
    chunk_decay = jnp.broadcast_to(jnp.exp(log_gamma * CHUNK)[:, None, None], (N_HEADS, 1, HEAD_DIM))
    full = lambda shp: pl.BlockSpec(shp, lambda b, i: (0,) * len(shp))
    return pl.pallas_call(
        _ret_kernel,
        out_shape=jax.ShapeDtypeStruct((bsz, t_len, GROUP_W), F32),
        grid=(bsz, t_len // tb),
        in_specs=[
            pl.BlockSpec((1, tb, 256), lambda b, i: (b, i, PD_Q // 256)),
            pl.BlockSpec((1, tb, 256), lambda b, i: (b, i, PD_K // 256)),
            pl.BlockSpec((1, tb, 512), lambda b, i: (b, i, PD_V // 512)),
            pl.BlockSpec((1, tb, 512), lambda b, i: (b, i, PD_G // 512)),
            full((N_HEADS, CHUNK, CHUNK)), full((N_HEADS, CHUNK, 1)), full((N_HEADS, CHUNK, 1)),
            full((N_HEADS, 1, HEAD_DIM)),
        ],
        out_specs=pl.BlockSpec((1, tb, GROUP_W), lambda b, i: (b, i, 0)),
        scratch_shapes=[pltpu.VMEM((N_HEADS, D_KEY_DIM, HEAD_DIM), F32)],
        compiler_params=_cparams(("parallel", "arbitrary")),
        name="retention",
    )(proj, proj, proj, proj, intra_decay, kv_decay, q_decay, chunk_decay)


def _deepnorm_ln(x, gate, branch, g, b):
    z = DEEPNORM_ALPHA * x + (1.0 + gate) * branch
    mu = jnp.mean(z, axis=-1, keepdims=True)
    d = z - mu
    var = jnp.mean(d * d, axis=-1, keepdims=True)
    return d * lax.rsqrt(var + LN_EPS) * g + b


def _outproj_kernel(ya_ref, yb_ref, yc_ref, yd_ref, w_ref, x_ref, gt_ref, g_ref, b_ref, o_ref):
    acc = None
    for gi, y_ref in enumerate((ya_ref, yb_ref, yc_ref, yd_ref)):
        part = _dot(y_ref[0].astype(BF16), w_ref[gi * GROUP_W:(gi + 1) * GROUP_W, :])
        acc = part if acc is None else acc + part
    o_ref[0] = _deepnorm_ln(x_ref[0], gt_ref[0], acc, g_ref[...], b_ref[...])


def _out_projection(ys, w, x, gt, ln_g, ln_b):
    bsz, t_len, d = x.shape
    tm = min(256, t_len)
    yspec = pl.BlockSpec((1, tm, GROUP_W), lambda b, i: (b, i, 0))
    xspec = pl.BlockSpec((1, tm, d), lambda b, i: (b, i, 0))
    vec = pl.BlockSpec((1, d), lambda b, i: (0, 0))
    return pl.pallas_call(
        _outproj_kernel,
        out_shape=jax.ShapeDtypeStruct(x.shape, F32),
        grid=(bsz, t_len // tm),
        in_specs=[yspec] * 4 + [pl.BlockSpec(w.shape, lambda b, i: (0, 0)), xspec,
                                pl.BlockSpec((1, 1, d), lambda b, i: (b, 0, 0)), vec, vec],
        out_specs=xspec,
        compiler_params=_cparams(("parallel", "parallel")),
        name="out_proj_ln",
    )(*ys, w, x, gt, ln_g.reshape(1, d), ln_b.reshape(1, d))


def _up_kernel(x_ref, sc_ref, sh_ref, wg_ref, wu_ref, cw_ref, cb_ref, o_ref, h_scr, halo_scr):
    i = pl.program_id(1)
    j = pl.program_id(2)

    @pl.when(j == 0)
    def _():
        h_scr[...] = (x_ref[0] * (1.0 + sc_ref[0]) + sh_ref[0]).astype(BF16)

    h = h_scr[...]
    gate = _dot(h, wg_ref[...])
    up = _dot(h, wu_ref[...])
    tm = gate.shape[0]

    @pl.when(i == 0)
    def _():
        halo_scr[j] = jnp.zeros(halo_scr.shape[1:], F32)

    halo = halo_scr[j]
    halo_scr[j] = gate[tm - SUBLANES:, :]
    rows = lax.broadcasted_iota(I32, gate.shape, 0)
    g1 = jnp.where(rows == 0, halo[SUBLANES - 1:SUBLANES, :], pltpu.roll(gate, 1, 0))
    g2 = jnp.where(rows == 0, halo[SUBLANES - 2:SUBLANES - 1, :],
                   jnp.where(rows == 1, halo[SUBLANES - 1:SUBLANES, :], pltpu.roll(gate, 2, 0)))
    cw = cw_ref[...]
    conv = cw[0:1, :] * g2 + cw[1:2, :] * g1 + cw[2:3, :] * gate + cb_ref[...]
    o_ref[0] = (_silu(conv) * up).astype(BF16)


def _up_projection(x, sc, sh, wg, wu, conv_w, conv_b):
    bsz, t_len, d = x.shape
    ff = wg.shape[1]
    tm = min(512, t_len)
    tn = 512
    return pl.pallas_call(
        _up_kernel,
        out_shape=jax.ShapeDtypeStruct((bsz, t_len, ff), BF16),
        grid=(bsz, t_len // tm, ff // tn),
        in_specs=[
            pl.BlockSpec((1, tm, d), lambda b, i, j: (b, i, 0)),
            pl.BlockSpec((1, 1, d), lambda b, i, j: (b, 0, 0)),
            pl.BlockSpec((1, 1, d), lambda b, i, j: (b, 0, 0)),
            pl.BlockSpec((d, tn), lambda b, i, j: (0, j)),
            pl.BlockSpec((d, tn), lambda b, i, j: (0, j)),
            pl.BlockSpec((CONV_W, tn), lambda b, i, j: (0, j)),
            pl.BlockSpec((1, tn), lambda b, i, j: (0, j)),
        ],
        out_specs=pl.BlockSpec((1, tm, tn), lambda b, i, j: (b, i, j)),
        scratch_shapes=[pltpu.VMEM((tm, d), BF16), pltpu.VMEM((ff // tn, SUBLANES, tn), F32)],
        compiler_params=_cparams(("parallel", "arbitrary", "arbitrary")),
        name="mlp_up_conv",
    )(x, sc, sh, wg, wu, conv_w, conv_b.reshape(1, ff))


def _down_kernel(f_ref, w_ref, x_ref, gt_ref, g_ref, b_ref, o_ref, acc_scr):
    kk = pl.program_id(2)

    @pl.when(kk == 0)
    def _():
        acc_scr[...] = jnp.zeros_like(acc_scr)

    acc_scr[...] += _dot(f_ref[0], w_ref[...])

    @pl.when(kk == pl.num_programs(2) - 1)
    def _():
        o_ref[0] = _deepnorm_ln(x_ref[0], gt_ref[0], acc_scr[...], g_ref[...], b_ref[...])


def _down_projection(f, w, x, gt, ln_g, ln_b):
    bsz, t_len, d = x.shape
    ff = f.shape[2]
    tm = min(512, t_len)
    tk = 512
    xspec = pl.BlockSpec((1, tm, d), lambda b, i, k: (b, i, 0))
    vec = pl.BlockSpec((1, d), lambda b, i, k: (0, 0))
    return pl.pallas_call(
        _down_kernel,
        out_shape=jax.ShapeDtypeStruct(x.shape, F32),
        grid=(bsz, t_len // tm, ff // tk),
        in_specs=[
            pl.BlockSpec((1, tm, tk), lambda b, i, k: (b, i, k)),
            pl.BlockSpec((tk, d), lambda b, i, k: (k, 0)),
            xspec,
            pl.BlockSpec((1, 1, d), lambda b, i, k: (b, 0, 0)),
            vec, vec,
        ],
        out_specs=xspec,
        scratch_shapes=[pltpu.VMEM((tm, d), F32)],
        compiler_params=_cparams(("parallel", "parallel", "arbitrary")),
        name="mlp_down_ln",
    )(f, w, x, gt, ln_g.reshape(1, d), ln_b.reshape(1, d))


def _pack_w_in(w):
    d = w.shape[0]
    a, b, c, dd = jnp.split(w, np.cumsum([A_COLS, B_COLS, C_COLS]).tolist(), axis=1)
    bq, bk, bv, bqi, bki, bwi = jnp.split(
        b, np.cumsum([GROUP_W, HEAD_DIM, HEAD_DIM, IDX_HEADS * IDX_DIM, IDX_DIM]).tolist(), axis=1)
    b_pad = jnp.zeros((d, 256 - 3 * HEAD_DIM - IDX_HEADS), w.dtype)
    a_pad = jnp.zeros((d, P_COLS - PA - A_COLS), w.dtype)
    packed = jnp.concatenate([c, dd, bq, bqi, bk, bv, bki, bwi, b_pad, a, a_pad], axis=1)
    assert packed.shape[1] == P_COLS
    return packed.astype(BF16)


def kernel(x, c, w_mod, b_mod, w_in, rwkv_mu, rwkv_w0, rwkv_w2, rwkv_a0, rwkv_a2, rwkv_g2,
           rwkv_kk, rwkv_ka, rwkv_rk, rwkv_ln_g, rwkv_ln_b, chunk_rel_bias, w_out, ln1_g, ln1_b,
           w_up, conv_w, conv_b, w_down, ln2_g, ln2_b):
    depth = w_mod.shape[0]
    d = x.shape[-1]
    mod = _modulation(c, w_mod, b_mod)
    for l in range(depth):
        sh1, sc1, gt1, sh2, sc2, gt2 = [mod[l, :, None, i * d:(i + 1) * d] for i in range(6)]
        proj = _in_projection(x, sc1, sh1, _pack_w_in(w_in[l]))
        r, w, k, v, a, b, g = _rwkv_prep(proj, rwkv_mu[l], rwkv_w0[l], rwkv_w2[l], rwkv_a0[l],
                                         rwkv_a2[l], rwkv_g2[l], rwkv_kk[l], rwkv_ka[l])
        y = _rwkv_scan(r, w, k, v, a, b)
        y_a = _rwkv_post(y, r, k, v, g, rwkv_ln_g[l], rwkv_ln_b[l], rwkv_rk[l])
        y_b = _dsa_mix(proj)
        y_c = _band_mix(proj, chunk_rel_bias[l])
        y_d = _ret_mix(proj)
        x = _out_projection((y_a, y_b, y_c, y_d), w_out[l].astype(BF16), x, gt1, ln1_g[l], ln1_b[l])
        ff = w_down.shape[1]
        f = _up_projection(x, sc2, sh2, w_up[l, :, :ff].astype(BF16), w_up[l, :, ff:].astype(BF16),
                           conv_w[l], conv_b[l])
        x = _down_projection(f, w_down[l].astype(BF16), x, gt2, ln2_g[l], ln2_b[l])
    return x
```

```python
import functools

import numpy as np
import jax
import jax.numpy as jnp
from jax import lax
from jax.experimental import pallas as pl
from jax.experimental.pallas import tpu as pltpu

F32 = jnp.float32
BF16 = jnp.bfloat16
I32 = jnp.int32
HIGHEST = lax.Precision.HIGHEST

D_MODEL = 2048
DEPTH = 4
CHUNK = 64
GROUP_W = 512
HEAD_DIM = 64
N_HEADS = 8
A_LORA_W, A_LORA_A, A_LORA_G = 64, 64, 128
A_GN_EPS = HEAD_DIM * 1e-5
IDX_HEADS, IDX_DIM = 4, 64
TOPK_MAX = 256
Q_BLOCK = 128
LEFT_CHUNKS = 8
REL_CLIP = 256
D_KEY_DIM = 32
RET_GN_EPS = 1e-5
D_FF = 5632
CONV_W = 3
LN_EPS = 1e-5
DEEPNORM_ALPHA = (2 * DEPTH) ** 0.25

A_COLS = 3 * GROUP_W + A_LORA_W + A_LORA_A + A_LORA_G
B_COLS = GROUP_W + 2 * HEAD_DIM + IDX_HEADS * IDX_DIM + IDX_DIM + IDX_HEADS
C_COLS = 3 * GROUP_W
D_COLS = 2 * N_HEADS * D_KEY_DIM + 2 * GROUP_W

LANES = 128
SUBLANES = 8
VMEM_LIMIT = 56 * 1024 * 1024

P_COLS = 6144
PC_Q, PC_K, PC_V = 0, 512, 1024
PD_Q, PD_K, PD_V, PD_G = 1536, 1792, 2048, 2560
PB_Q, PB_QI, PB_KV = 3072, 3584, 3840
PA = 4096
INT_MIN = -2 ** 31


def _cparams(sem):
    return pltpu.CompilerParams(dimension_semantics=sem, vmem_limit_bytes=VMEM_LIMIT)


def _sigmoid(x):
    return 1.0 / (1.0 + jnp.exp(-x))


def _silu(x):
    return x * _sigmoid(x)


def _dot(a, b, **kw):
    return jnp.dot(a, b, preferred_element_type=F32, **kw)


def _dot_nt(a, b, **kw):
    return lax.dot_general(a, b, (((1,), (1,)), ((), ())), preferred_element_type=F32, **kw)


def _mod_kernel(c_ref, w_ref, b_ref, o_ref):
    ca = _silu(c_ref[...])
    o_ref[0] = _dot(ca, w_ref[0], precision=HIGHEST) + b_ref[0]


def _modulation(c, w_mod, b_mod):
    depth, d, n6 = w_mod.shape
    bsz = c.shape[0]
    tn = 1024
    return pl.pallas_call(
        _mod_kernel,
        out_shape=jax.ShapeDtypeStruct((depth, bsz, n6), F32),
        grid=(depth, n6 // tn),
        in_specs=[
            pl.BlockSpec((bsz, d), lambda l, j: (0, 0)),
            pl.BlockSpec((1, d, tn), lambda l, j: (l, 0, j)),
            pl.BlockSpec((1, 1, tn), lambda l, j: (l, 0, j)),
        ],
        out_specs=pl.BlockSpec((1, bsz, tn), lambda l, j: (l, 0, j)),
        compiler_params=_cparams(("parallel", "parallel")),
        name="adaln_mod",
    )(c, w_mod, b_mod.reshape(depth, 1, n6))


def _inproj_kernel(x_ref, sc_ref, sh_ref, w_ref, o_ref, h_scr):
    @pl.when(pl.program_id(2) == 0)
    def _():
        h_scr[...] = (x_ref[0] * (1.0 + sc_ref[0]) + sh_ref[0]).astype(BF16)

    o_ref[0] = _dot(h_scr[...], w_ref[...])


def _in_projection(x, sc, sh, w):
    bsz, t_len, d = x.shape
    n = w.shape[1]
    tm = min(512, t_len)
    tn = 512
    return pl.pallas_call(
        _inproj_kernel,
        out_shape=jax.ShapeDtypeStruct((bsz, t_len, n), F32),
        grid=(bsz, t_len // tm, n // tn),
        in_specs=[
            pl.BlockSpec((1, tm, d), lambda b, i, j: (b, i, 0)),
            pl.BlockSpec((1, 1, d), lambda b, i, j: (b, 0, 0)),
            pl.BlockSpec((1, 1, d), lambda b, i, j: (b, 0, 0)),
            pl.BlockSpec((d, tn), lambda b, i, j: (0, j)),
        ],
        out_specs=pl.BlockSpec((1, tm, tn), lambda b, i, j: (b, i, j)),
        scratch_shapes=[pltpu.VMEM((tm, d), BF16)],
        compiler_params=_cparams(("parallel", "parallel", "arbitrary")),
        name="in_proj",
    )(x, sc, sh, w)


def _head_segment_ones():
    seg = (np.arange(GROUP_W)[:, None] // HEAD_DIM == np.arange(GROUP_W)[None, :] // HEAD_DIM)
    return jnp.asarray(seg, F32)


def _rwkv_prep_kernel(p_ref, prev_ref, mu_ref, w0_ref, w2_ref, a0_ref, a2_ref, g2_ref,
                      kk_ref, ka_ref, seg_ref,
                      r_out, w_out, k_out, v_out, a_out, b_out, g_out):
    i = pl.program_id(1)
    p = p_ref[0][:, :A_COLS]
    last_prev = prev_ref[0][SUBLANES - 1:SUBLANES, :A_COLS]
    last_prev = jnp.where(i == 0, 0.0, last_prev)
    rows = lax.broadcasted_iota(I32, p.shape, 0)
    shifted = jnp.where(rows == 0, last_prev, pltpu.roll(p, 1, 0))
    xs = p + (shifted - p) * mu_ref[...]
    r = xs[:, 0:512]
    k = xs[:, 512:1024]
    v = xs[:, 1024:1536]
    wl = xs[:, 1536:1600]
    al = xs[:, 1600:1664]
    gl = xs[:, 1664:1792]
    z = w0_ref[...] + _dot(jnp.tanh(wl), w2_ref[...], precision=HIGHEST)
    nz = -z
    softplus = jnp.maximum(nz, 0.0) + jnp.log1p(jnp.exp(-jnp.abs(nz)))
    w_log = -softplus - 0.5
    a = _sigmoid(a0_ref[...] + _dot(al, a2_ref[...], precision=HIGHEST))
    kk = k * kk_ref[...]
    ssq = _dot(kk * kk, seg_ref[...], precision=HIGHEST)
    kk = kk / jnp.maximum(jnp.sqrt(ssq), 1e-12)
    r_out[0] = r
    w_out[0] = jnp.exp(-jnp.exp(w_log))
    k_out[0] = k * (1.0 + (a - 1.0) * ka_ref[...])
    v_out[0] = v
    a_out[0] = -kk
    b_out[0] = kk * a
    g_out[0] = _dot(_sigmoid(gl), g2_ref[...], precision=HIGHEST)


def _rwkv_prep(proj, mu, w0, w2, a0, a2, g2, k_k, k_a):
    bsz, t_len, _ = proj.shape
    tb = min(256, t_len)
    a_blk = PA // 2048
    row = lambda z: z.reshape(1, -1)
    full = lambda shp: pl.BlockSpec(shp, lambda b, i: (0,) * len(shp))
    out = jax.ShapeDtypeStruct((bsz, t_len, GROUP_W), F32)
    out_spec = pl.BlockSpec((1, tb, GROUP_W), lambda b, i: (b, i, 0))
    return pl.pallas_call(
        _rwkv_prep_kernel,
        out_shape=[out] * 7,
        grid=(bsz, t_len // tb),
        in_specs=[
            pl.BlockSpec((1, tb, 2048), lambda b, i: (b, i, a_blk)),
            pl.BlockSpec((1, SUBLANES, 2048),
                         lambda b, i: (b, jnp.maximum(i * (tb // SUBLANES) - 1, 0), a_blk)),
            full((1, A_COLS)), full((1, GROUP_W)), full((A_LORA_W, GROUP_W)),
            full((1, GROUP_W)), full((A_LORA_A, GROUP_W)), full((A_LORA_G, GROUP_W)),
            full((1, GROUP_W)), full((1, GROUP_W)), full((GROUP_W, GROUP_W)),
        ],
        out_specs=[out_spec] * 7,
        compiler_params=_cparams(("parallel", "parallel")),
        name="rwkv_prep",
    )(proj, proj, row(mu), row(w0), w2, row(a0), a2, g2, row(k_k), row(k_a), _head_segment_ones())


SCAN_CHAINS = LANES // 2
SCAN_KH = HEAD_DIM // 2
SCAN_TB = 64


def _key_sum(x):
    s = jnp.sum(x, axis=0, keepdims=True)
    return s + pltpu.roll(s, SCAN_CHAINS, 1)


def _rwkv_scan_kernel(r_ref, w_ref, k_ref, v_ref, a_ref, b_ref, y_ref, s_scr):
    @pl.when(pl.program_id(1) == 0)
    def _():
        s_scr[...] = jnp.zeros_like(s_scr)

    tb = r_ref.shape[1]

    def step(t, carry):
        r_t, w_t, k_t, a_t, b_t = (ref[0, t] for ref in (r_ref, w_ref, k_ref, a_ref, b_ref))
        for v in range(HEAD_DIM):
            s = s_scr[v]
            sa = _key_sum(s * a_t)
            s = s * w_t + sa * b_t + v_ref[0, t, v:v + 1, :] * k_t
            s_scr[v] = s
            y_ref[0, t, v:v + 1, :] = _key_sum(s * r_t)
        return carry

    lax.fori_loop(0, tb, step, 0)


def _to_scan_layout(z, key_split):
    bsz, t_len, _ = z.shape
    pad = (-bsz) % (SCAN_CHAINS // N_HEADS)
    if pad:
        z = jnp.pad(z, ((0, pad), (0, 0), (0, 0)))
    g = z.shape[0] * N_HEADS // SCAN_CHAINS
    bg = SCAN_CHAINS // N_HEADS
    if key_split:
        z = z.reshape(g, bg, t_len, N_HEADS, 2, SCAN_KH)
        z = jnp.transpose(z, (0, 2, 5, 4, 1, 3))
        return z.reshape(g, t_len, SCAN_KH, LANES)
    z = z.reshape(g, bg, t_len, N_HEADS, HEAD_DIM)
    z = jnp.transpose(z, (0, 2, 4, 1, 3)).reshape(g, t_len, HEAD_DIM, SCAN_CHAINS)
    return jnp.concatenate([z, z], axis=-1)


def _from_scan_layout(y, bsz):
    g, t_len = y.shape[:2]
    bg = SCAN_CHAINS // N_HEADS
    y = y[..., :SCAN_CHAINS].reshape(g, t_len, HEAD_DIM, bg, N_HEADS)
    y = jnp.transpose(y, (0, 3, 1, 4, 2)).reshape(g * bg, t_len, GROUP_W)
    return y[:bsz]


def _rwkv_scan(r, w, k, v, a, b):
    bsz, t_len, _ = r.shape
    rs, ws, ks, as_, bs = (_to_scan_layout(z, True) for z in (r, w, k, a, b))
    vs = _to_scan_layout(v, False)
    g = rs.shape[0]
    tb = min(SCAN_TB, t_len)
    kspec = pl.BlockSpec((1, tb, SCAN_KH, LANES), lambda gi, i: (gi, i, 0, 0))
    vspec = pl.BlockSpec((1, tb, HEAD_DIM, LANES), lambda gi, i: (gi, i, 0, 0))
    y = pl.pallas_call(
        _rwkv_scan_kernel,
        out_shape=jax.ShapeDtypeStruct((g, t_len, HEAD_DIM, LANES), F32),
        grid=(g, t_len // tb),
        in_specs=[kspec, kspec, kspec, vspec, kspec, kspec],
        out_specs=vspec,
        scratch_shapes=[pltpu.VMEM((HEAD_DIM, SCAN_KH, LANES), F32)],
        compiler_params=_cparams(("parallel", "arbitrary")),
        name="rwkv_scan",
    )(rs, ws, ks, vs, as_, bs)
    return _from_scan_layout(y, bsz)


def _rwkv_post_kernel(y_ref, r_ref, k_ref, v_ref, g_ref, lng_ref, lnb_ref, rk_ref, seg_ref, o_ref):
    seg = seg_ref[...]
    y = y_ref[0]
    mu = _dot(y, seg, precision=HIGHEST) * (1.0 / HEAD_DIM)
    d = y - mu
    var = _dot(d * d, seg, precision=HIGHEST) * (1.0 / HEAD_DIM)
    yn = d * lax.rsqrt(var + A_GN_EPS) * lng_ref[...] + lnb_ref[...]
    bonus = _dot(r_ref[0] * k_ref[0] * rk_ref[...], seg, precision=HIGHEST) * v_ref[0]
    o_ref[0] = (yn + bonus) * g_ref[0]


def _rwkv_post(y, r, k, v, g, ln_g, ln_b, r_k):
    bsz, t_len, _ = y.shape
    tb = min(256, t_len)
    spec = pl.BlockSpec((1, tb, GROUP_W), lambda b, i: (b, i, 0))
    par = pl.BlockSpec((1, GROUP_W), lambda b, i: (0, 0))
    row = lambda z: z.reshape(1, GROUP_W)
    return pl.pallas_call(
        _rwkv_post_kernel,
        out_shape=jax.ShapeDtypeStruct((bsz, t_len, GROUP_W), F32),
        grid=(bsz, t_len // tb),
        in_specs=[spec] * 5 + [par, par, par, pl.BlockSpec((GROUP_W, GROUP_W), lambda b, i: (0, 0))],
        out_specs=spec,
        compiler_params=_cparams(("parallel", "parallel")),
        name="rwkv_post",
    )(y, r, k, v, g, row(ln_g), row(ln_b), row(r_k), _head_segment_ones())


DSA_KC = 512
NEG_BIG = -1e30


def _split_bf16(x):
    hi = x.astype(BF16)
    lo = (x - hi.astype(F32)).astype(BF16)
    return hi, lo


def _dsa_kernel(q_ref, qi_ref, kv_ref, o_ref, key_scr, bias_scr, *, topk):
    n = pl.program_id(1)
    t_len = kv_ref.shape[1]
    nq = Q_BLOCK
    kc = min(DSA_KC, t_len)
    n_chunks = (n * nq + nq + kc - 1) // kc
    t_row = n * nq + lax.broadcasted_iota(I32, (nq, 1), 0)
    limit = (t_row // CHUNK + 1) * CHUNK

    qi = qi_ref[0]
    qi_rows = jnp.concatenate([qi[:, h * IDX_DIM:(h + 1) * IDX_DIM] for h in range(IDX_HEADS)], axis=0)
    qi_hi, qi_lo = _split_bf16(qi_rows)
    wi = kv_ref[0, pl.ds(pl.multiple_of(n * nq, nq), nq), 3 * HEAD_DIM:3 * HEAD_DIM + IDX_HEADS]
    wi = wi * (IDX_HEADS ** -0.5 * IDX_DIM ** -0.5)

    def score_chunk(c, carry):
        k0 = pl.multiple_of(c * kc, kc)
        ki_hi, ki_lo = _split_bf16(kv_ref[0, pl.ds(k0, kc), 2 * HEAD_DIM:3 * HEAD_DIM])
        lg = _dot_nt(qi_hi, ki_hi) + (_dot_nt(qi_hi, ki_lo) + _dot_nt(qi_lo, ki_hi))
        score = None
        for h in range(IDX_HEADS):
            term = jnp.maximum(lg[h * nq:(h + 1) * nq], 0.0) * wi[:, h:h + 1]
            score = term if score is None else score + term
        score = jnp.where(score == 0.0, 0.0, score)
        bits = pltpu.bitcast(score, I32)
        okey = jnp.where(bits >= 0, bits, bits ^ jnp.int32(0x7FFFFFFF))
        pos = k0 + lax.broadcasted_iota(I32, (nq, kc), 1)
        key_scr[c] = jnp.where(pos < limit, okey, jnp.int32(INT_MIN))
        return carry

    lax.fori_loop(0, n_chunks, score_chunk, 0)

    def count_ge(cand):
        def body(c, acc):
            keys = key_scr[c]
            for s in range(kc // LANES):
                acc = acc + jnp.where(keys[:, s * LANES:(s + 1) * LANES] >= cand, 1.0, 0.0)
            return acc
        acc = lax.fori_loop(0, n_chunks, body, jnp.zeros((nq, LANES), F32))
        return jnp.broadcast_to(jnp.sum(acc, axis=-1, keepdims=True), (nq, LANES))

    kf = jnp.float32(topk)
    m0 = jnp.where(count_ge(jnp.zeros((nq, LANES), I32)) >= kf, jnp.int32(0), jnp.int32(INT_MIN))

    def bit_step(i, m):
        cand = m | (jnp.int32(1) << (jnp.int32(30) - i))
        return jnp.where(count_ge(cand) >= kf, cand, m)

    thr = lax.fori_loop(0, 31, bit_step, m0)
    cnt_gt = count_ge(thr + 1)
    need = kf - cnt_gt

    tri = (lax.broadcasted_iota(I32, (LANES, LANES), 0)
           <= lax.broadcasted_iota(I32, (LANES, LANES), 1)).astype(BF16)

    def select_chunk(c, running):
        keys = key_scr[c]
        parts = []
        for s in range(kc // LANES):
            kt = keys[:, s * LANES:(s + 1) * LANES]
            tie = kt == thr
            prefix = _dot(jnp.where(tie, 1.0, 0.0).astype(BF16), tri)
            rank = running + prefix
            sel = (kt > thr) | (tie & (rank <= need) & (kt != jnp.int32(INT_MIN)))
            parts.append(jnp.where(sel, 0.0, -jnp.inf))
            running = running + jnp.broadcast_to(prefix[:, LANES - 1:LANES], (nq, LANES))
        bias_scr[c] = jnp.concatenate(parts, axis=-1)
        return running

    lax.fori_loop(0, n_chunks, select_chunk, jnp.zeros((nq, LANES), F32))

    q = q_ref[0] * (HEAD_DIM ** -0.5)
    q_rows = jnp.concatenate([q[:, h * HEAD_DIM:(h + 1) * HEAD_DIM] for h in range(N_HEADS)],
                             axis=0).astype(BF16)
    rows = N_HEADS * nq
    head = lax.broadcasted_iota(I32, (rows, 1), 0) // nq
    slope = pltpu.bitcast((126 - head) << 23, F32)

    def attn_chunk(c, carry):
        m, l, acc = carry
        k0 = pl.multiple_of(c * kc, kc)
        kk = kv_ref[0, pl.ds(k0, kc), 0:HEAD_DIM].astype(BF16)
        vv = kv_ref[0, pl.ds(k0, kc), HEAD_DIM:2 * HEAD_DIM].astype(BF16)
        s = _dot_nt(q_rows, kk)
        pos = k0 + lax.broadcasted_iota(I32, (nq, kc), 1)
        dist = jnp.concatenate([jnp.abs(t_row - pos).astype(F32)] * N_HEADS, axis=0)
        bias = jnp.concatenate([bias_scr[c]] * N_HEADS, axis=0)
        s = s - slope * dist + bias
        m_new = jnp.maximum(m, jnp.max(s, axis=-1, keepdims=True))
        alpha = jnp.exp(m - m_new)
        p = jnp.exp(s - m_new)
        l = alpha * l + jnp.sum(p, axis=-1, keepdims=True)
        acc = alpha * acc + _dot(p.astype(BF16), vv)
        return m_new, l, acc

    _, l, acc = lax.fori_loop(0, n_chunks, attn_chunk,
                              (jnp.full((rows, 1), NEG_BIG, F32), jnp.zeros((rows, 1), F32),
                               jnp.zeros((rows, HEAD_DIM), F32)))
    out = acc / l
    o_ref[0] = jnp.concatenate([out[h * nq:(h + 1) * nq] for h in range(N_HEADS)], axis=-1)


def _dsa_mix(proj):
    bsz, t_len, _ = proj.shape
    topk = min(TOPK_MAX, t_len // 4)
    kc = min(DSA_KC, t_len)
    nck = t_len // kc
    return pl.pallas_call(
        functools.partial(_dsa_kernel, topk=topk),
        out_shape=jax.ShapeDtypeStruct((bsz, t_len, GROUP_W), F32),
        grid=(bsz, t_len // Q_BLOCK),
        in_specs=[
            pl.BlockSpec((1, Q_BLOCK, 512), lambda b, n: (b, n, PB_Q // 512)),
            pl.BlockSpec((1, Q_BLOCK, 256), lambda b, n: (b, n, PB_QI // 256)),
            pl.BlockSpec((1, t_len, 256), lambda b, n: (b, 0, PB_KV // 256)),
        ],
        out_specs=pl.BlockSpec((1, Q_BLOCK, GROUP_W), lambda b, n: (b, n, 0)),
        scratch_shapes=[pltpu.VMEM((nck, Q_BLOCK, kc), I32),
                        pltpu.VMEM((nck, Q_BLOCK, kc), F32)],
        compiler_params=_cparams(("parallel", "arbitrary")),
        name="dsa_mix",
    )(proj, proj, proj)


BAND_TQ = LEFT_CHUNKS * CHUNK
BAND_W = (LEFT_CHUNKS + 1) * CHUNK


def _band_kernel(q_ref, kp_ref, kc_ref, vp_ref, vc_ref, bias_ref, o_ref):
    i = pl.program_id(1)
    tq = q_ref.shape[1]
    kwin = jnp.concatenate([kp_ref[0], kc_ref[0]], axis=0).astype(BF16)
    vwin = jnp.concatenate([vp_ref[0], vc_ref[0]], axis=0).astype(BF16)
    col = lax.broadcasted_iota(I32, (CHUNK, BAND_W), 1)
    for c in range(tq // CHUNK):
        r0 = c * CHUNK
        qc = q_ref[0, r0:r0 + CHUNK, :].astype(BF16)
        kc_ = kwin[r0:r0 + BAND_W]
        vc_ = vwin[r0:r0 + BAND_W]
        valid = jnp.logical_or(i > 0, col >= tq - r0)
        outs = []
        for h in range(N_HEADS):
            sl = slice(h * HEAD_DIM, (h + 1) * HEAD_DIM)
            s = _dot_nt(qc[:, sl], kc_[:, sl]) * (HEAD_DIM ** -0.5) + bias_ref[h]
            s = jnp.where(valid, s, -jnp.inf)
            mx = jnp.max(s, axis=-1, keepdims=True)
            p = jnp.exp(s - mx)
            l = jnp.sum(p, axis=-1, keepdims=True)
            outs.append(_dot(p.astype(BF16), vc_[:, sl]) / l)
        o_ref[0, r0:r0 + CHUNK, :] = jnp.concatenate(outs, axis=-1)


def _band_mix(proj, rel_bias):
    bsz, t_len, _ = proj.shape
    tq = BAND_TQ
    i_ = np.arange(CHUNK)[:, None]
    j_ = np.arange(BAND_W)[None, :]
    rel = np.clip(i_ - j_ + LEFT_CHUNKS * CHUNK, -REL_CLIP, REL_CLIP) + REL_CLIP
    bias = rel_bias[:, rel].astype(F32)
    cur = lambda blk: pl.BlockSpec((1, tq, 512), lambda b, i: (b, i, blk))
    prev = lambda blk: pl.BlockSpec((1, tq, 512), lambda b, i: (b, jnp.maximum(i - 1, 0), blk))
    return pl.pallas_call(
        _band_kernel,
        out_shape=jax.ShapeDtypeStruct((bsz, t_len, GROUP_W), F32),
        grid=(bsz, t_len // tq),
        in_specs=[cur(PC_Q // 512), prev(PC_K // 512), cur(PC_K // 512),
                  prev(PC_V // 512), cur(PC_V // 512),
                  pl.BlockSpec((N_HEADS, CHUNK, BAND_W), lambda b, i: (0, 0, 0))],
        out_specs=pl.BlockSpec((1, tq, GROUP_W), lambda b, i: (b, i, 0)),
        compiler_params=_cparams(("parallel", "parallel")),
        name="band_attn",
    )(proj, proj, proj, proj, proj, bias)


RET_TB = 512


def _ret_kernel(q_ref, k_ref, v_ref, g_ref, intra_ref, kvd_ref, qd_ref, cd_ref, o_ref, st_scr):
    @pl.when(pl.program_id(1) == 0)
    def _():
        st_scr[...] = jnp.zeros_like(st_scr)

    tb = q_ref.shape[1]
    eye = (lax.broadcasted_iota(I32, (D_KEY_DIM, D_KEY_DIM), 0)
           == lax.broadcasted_iota(I32, (D_KEY_DIM, D_KEY_DIM), 1)).astype(BF16)
    for c in range(tb // CHUNK):
        r0 = c * CHUNK
        q = q_ref[0, r0:r0 + CHUNK, :]
        k = k_ref[0, r0:r0 + CHUNK, :] * (D_KEY_DIM ** -0.5)
        v = v_ref[0, r0:r0 + CHUNK, :]
        g = g_ref[0, r0:r0 + CHUNK, :]
        outs = []
        for h in range(N_HEADS):
            ks = slice(h * D_KEY_DIM, (h + 1) * D_KEY_DIM)
            vs = slice(h * HEAD_DIM, (h + 1) * HEAD_DIM)
            qh, kh, vh = q[:, ks], k[:, ks], v[:, vs]
            vb = vh.astype(BF16)
            s = _dot_nt(qh.astype(BF16), kh.astype(BF16)) * intra_ref[h]
            intra = _dot(s.astype(BF16), vb)
            state = st_scr[h]
            inter = _dot((qh * qd_ref[h]).astype(BF16), state.astype(BF16))
            kd_t = _dot_nt(eye, (kh * kvd_ref[h]).astype(BF16)).astype(BF16)
            st_scr[h] = state * cd_ref[h] + _dot(kd_t, vb)
            y = intra + inter
            mu = jnp.mean(y, axis=-1, keepdims=True)
            d = y - mu
            var = jnp.mean(d * d, axis=-1, keepdims=True)
            outs.append(d * lax.rsqrt(var + RET_GN_EPS))
        o_ref[0, r0:r0 + CHUNK, :] = jnp.concatenate(outs, axis=-1) * _silu(g)


def _ret_mix(proj):
    bsz, t_len, _ = proj.shape
    tb = min(RET_TB, t_len)
    log_gamma = jnp.log1p(-(2.0 ** (-5.0 - jnp.arange(N_HEADS, dtype=F32))))
    pos = jnp.arange(CHUNK, dtype=F32)
    diff = pos[:, None] - pos[None, :]
    intra_decay = jnp.where(diff >= 0, jnp.exp(log_gamma[:, None, None] * jnp.maximum(diff, 0.0)), 0.0)
    kv_decay = jnp.exp(log_gamma[:, None] * (CHUNK - 1 - pos)[None, :])[:, :, None]
    q_decay = jnp.exp(log_gamma[:, None] * (pos + 1)[None, :])[:, :, None]
    chunk_decay = jnp.broadcast_to(jnp.exp(log_gamma * CHUNK)[:, None, None], (N_HEADS, 1, HEAD_DIM))
    full = lambda shp: pl.BlockSpec(shp, lambda b, i: (0,) * len(shp))
    return pl.pallas_call(
        _ret_kernel,
        out_shape=jax.ShapeDtypeStruct((bsz, t_len, GROUP_W), F32),
        grid=(bsz, t_len // tb),
        in_specs=[
            pl.BlockSpec((1, tb, 256), lambda b, i: (b, i, PD_Q // 256)),
            pl.BlockSpec((1, tb, 256), lambda b, i: (b, i, PD_K // 256)),
            pl.BlockSpec((1, tb, 512), lambda b, i: (b, i, PD_V // 512)),
            pl.BlockSpec((1, tb, 512), lambda b, i: (b, i, PD_G // 512)),
            full((N_HEADS, CHUNK, CHUNK)), full((N_HEADS, CHUNK, 1)), full((N_HEADS, CHUNK, 1)),
            full((N_HEADS, 1, HEAD_DIM)),
        ],
        out_specs=pl.BlockSpec((1, tb, GROUP_W), lambda b, i: (b, i, 0)),
        scratch_shapes=[pltpu.VMEM((N_HEADS, D_KEY_DIM, HEAD_DIM), F32)],
        compiler_params=_cparams(("parallel", "arbitrary")),
        name="retention",
    )(proj, proj, proj, proj, intra_decay, kv_decay, q_decay, chunk_decay)


def _deepnorm_ln(x, gate, branch, g, b):
    z = DEEPNORM_ALPHA * x + (1.0 + gate) * branch
    mu = jnp.mean(z, axis=-1, keepdims=True)
    d = z - mu
    var = jnp.mean(d * d, axis=-1, keepdims=True)
    return d * lax.rsqrt(var + LN_EPS) * g + b


def _outproj_kernel(ya_ref, yb_ref, yc_ref, yd_ref, w_ref, x_ref, gt_ref, g_ref, b_ref, o_ref):
    acc = None
    for gi, y_ref in enumerate((ya_ref, yb_ref, yc_ref, yd_ref)):
        part = _dot(y_ref[0].astype(BF16), w_ref[gi * GROUP_W:(gi + 1) * GROUP_W, :])
        acc = part if acc is None else acc + part
    o_ref[0] = _deepnorm_ln(x_ref[0], gt_ref[0], acc, g_ref[...], b_ref[...])


def _out_projection(ys, w, x, gt, ln_g, ln_b):
    bsz, t_len, d = x.shape
    tm = min(256, t_len)
    yspec = pl.BlockSpec((1, tm, GROUP_W), lambda b, i: (b, i, 0))
    xspec = pl.BlockSpec((1, tm, d), lambda b, i: (b, i, 0))
    vec = pl.BlockSpec((1, d), lambda b, i: (0, 0))
    return pl.pallas_call(
        _outproj_kernel,
        out_shape=jax.ShapeDtypeStruct(x.shape, F32),
        grid=(bsz, t_len // tm),
        in_specs=[yspec] * 4 + [pl.BlockSpec(w.shape, lambda b, i: (0, 0)), xspec,
                                pl.BlockSpec((1, 1, d), lambda b, i: (b, 0, 0)), vec, vec],
        out_specs=xspec,
        compiler_params=_cparams(("parallel", "parallel")),
        name="out_proj_ln",
    )(*ys, w, x, gt, ln_g.reshape(1, d), ln_b.reshape(1, d))


def _up_kernel(x_ref, sc_ref, sh_ref, wg_ref, wu_ref, cw_ref, cb_ref, o_ref, h_scr, halo_scr):
    i = pl.program_id(1)
    j = pl.program_id(2)

    @pl.when(j == 0)
    def _():
        h_scr[...] = (x_ref[0] * (1.0 + sc_ref[0]) + sh_ref[0]).astype(BF16)

    h = h_scr[...]
    gate = _dot(h, wg_ref[...])
    up = _dot(h, wu_ref[...])
    tm = gate.shape[0]

    @pl.when(i == 0)
    def _():
        halo_scr[j] = jnp.zeros(halo_scr.shape[1:], F32)

    halo = halo_scr[j]
    halo_scr[j] = gate[tm - SUBLANES:, :]
    rows = lax.broadcasted_iota(I32, gate.shape, 0)
    g1 = jnp.where(rows == 0, halo[SUBLANES - 1:SUBLANES, :], pltpu.roll(gate, 1, 0))
    g2 = jnp.where(rows == 0, halo[SUBLANES - 2:SUBLANES - 1, :],
                   jnp.where(rows == 1, halo[SUBLANES - 1:SUBLANES, :], pltpu.roll(gate, 2, 0)))
    cw = cw_ref[...]
    conv = cw[0:1, :] * g2 + cw[1:2, :] * g1 + cw[2:3, :] * gate + cb_ref[...]
    o_ref[0] = (_silu(conv) * up).astype(BF16)


def _up_projection(x, sc, sh, wg, wu, conv_w, conv_b):
    bsz, t_len, d = x.shape
    ff = wg.shape[1]
    tm = min(512, t_len)
    tn = 512
    return pl.pallas_call(
        _up_kernel,
        out_shape=jax.ShapeDtypeStruct((bsz, t_len, ff), BF16),
        grid=(bsz, t_len // tm, ff // tn),
        in_specs=[
            pl.BlockSpec((1, tm, d), lambda b, i, j: (b, i, 0)),
            pl.BlockSpec((1, 1, d), lambda b, i, j: (b, 0, 0)),
            pl.BlockSpec((1, 1, d), lambda b, i, j: (b, 0, 0)),
            pl.BlockSpec((d, tn), lambda b, i, j: (0, j)),
            pl.BlockSpec((d, tn), lambda b, i, j: (0, j)),
            pl.BlockSpec((CONV_W, tn), lambda b, i, j: (0, j)),
            pl.BlockSpec((1, tn), lambda b, i, j: (0, j)),
        ],
        out_specs=pl.BlockSpec((1, tm, tn), lambda b, i, j: (b, i, j)),
        scratch_shapes=[pltpu.VMEM((tm, d), BF16), pltpu.VMEM((ff // tn, SUBLANES, tn), F32)],
        compiler_params=_cparams(("parallel", "arbitrary", "arbitrary")),
        name="mlp_up_conv",
    )(x, sc, sh, wg, wu, conv_w, conv_b.reshape(1, ff))


def _down_kernel(f_ref, w_ref, x_ref, gt_ref, g_ref, b_ref, o_ref, acc_scr):
    kk = pl.program_id(2)

    @pl.when(kk == 0)
    def _():
        acc_scr[...] = jnp.zeros_like(acc_scr)

    acc_scr[...] += _dot(f_ref[0], w_ref[...])

    @pl.when(kk == pl.num_programs(2) - 1)
    def _():
        o_ref[0] = _deepnorm_ln(x_ref[0], gt_ref[0], acc_scr[...], g_ref[...], b_ref[...])


def _down_projection(f, w, x, gt, ln_g, ln_b):
    bsz, t_len, d = x.shape
    ff = f.shape[2]
    tm = min(512, t_len)
    tk = 512
    xspec = pl.BlockSpec((1, tm, d), lambda b, i, k: (b, i, 0))
    vec = pl.BlockSpec((1, d), lambda b, i, k: (0, 0))
    return pl.pallas_call(
        _down_kernel,
        out_shape=jax.ShapeDtypeStruct(x.shape, F32),
        grid=(bsz, t_len // tm, ff // tk),
        in_specs=[
            pl.BlockSpec((1, tm, tk), lambda b, i, k: (b, i, k)),
            pl.BlockSpec((tk, d), lambda b, i, k: (k, 0)),
            xspec,
            pl.BlockSpec((1, 1, d), lambda b, i, k: (b, 0, 0)),
            vec, vec,
        ],
        out_specs=xspec,
        scratch_shapes=[pltpu.VMEM((tm, d), F32)],
        compiler_params=_cparams(("parallel", "parallel", "arbitrary")),
        name="mlp_down_ln",
    )(f, w, x, gt, ln_g.reshape(1, d), ln_b.reshape(1, d))


def _pack_w_in(w):
    d = w.shape[0]
    a, b, c, dd = jnp.split(w, np.cumsum([A_COLS, B_COLS, C_COLS]).tolist(), axis=1)
    bq, bk, bv, bqi, bki, bwi = jnp.split(
        b, np.cumsum([GROUP_W, HEAD_DIM, HEAD_DIM, IDX_HEADS * IDX_DIM, IDX_DIM]).tolist(), axis=1)
    b_pad = jnp.zeros((d, 256 - 3 * HEAD_DIM - IDX_HEADS), w.dtype)
    a_pad = jnp.zeros((d, P_COLS - PA - A_COLS), w.dtype)
    packed = jnp.concatenate([c, dd, bq, bqi, bk, bv, bki, bwi, b_pad, a, a_pad], axis=1)
    assert packed.shape[1] == P_COLS
    return packed.astype(BF16)


def kernel(x, c, w_mod, b_mod, w_in, rwkv_mu, rwkv_w0, rwkv_w2, rwkv_a0, rwkv_a2, rwkv_g2,
           rwkv_kk, rwkv_ka, rwkv_rk, rwkv_ln_g, rwkv_ln_b, chunk_rel_bias, w_out, ln1_g, ln1_b,
           w_up, conv_w, conv_b, w_down, ln2_g, ln2_b):
    depth = w_mod.shape[0]
    d = x.shape[-1]
    mod = _modulation(c, w_mod, b_mod)
    for l in range(depth):
        sh1, sc1, gt1, sh2, sc2, gt2 = [mod[l, :, None, i * d:(i + 1) * d] for i in range(6)]
        proj = _in_projection(x, sc1, sh1, _pack_w_in(w_in[l]))
        r, w, k, v, a, b, g = _rwkv_prep(proj, rwkv_mu[l], rwkv_w0[l], rwkv_w2[l], rwkv_a0[l],
                                         rwkv_a2[l], rwkv_g2[l], rwkv_kk[l], rwkv_ka[l])
        y = _rwkv_scan(r, w, k, v, a, b)
        y_a = _rwkv_post(y, r, k, v, g, rwkv_ln_g[l], rwkv_ln_b[l], rwkv_rk[l])
        y_b = _dsa_mix(proj)
        y_c = _band_mix(proj, chunk_rel_bias[l])
        y_d = _ret_mix(proj)
        x = _out_projection((y_a, y_b, y_c, y_d), w_out[l].astype(BF16), x, gt1, ln1_g[l], ln1_b[l])
        ff = w_down.shape[1]
        f = _up_projection(x, sc2, sh2, w_up[l, :, :ff].astype(BF16), w_up[l, :, ff:].astype(BF16),
                           conv_w[l], conv_b[l])
        x = _down_projection(f, w_down[l].astype(BF16), x, gt2, ln2_g[l], ln2_b[l])
    return x
```

```python
import functools

import numpy as np
import jax
import jax.numpy as jnp
from jax import lax
from jax.experimental import pallas as pl
from jax.experimental.pallas import tpu as pltpu

F32 = jnp.float32
BF16 = jnp.bfloat16
I32 = jnp.int32
HIGHEST = lax.Precision.HIGHEST

D_MODEL = 2048
DEPTH = 4
CHUNK = 64
GROUP_W = 512
HEAD_DIM = 64
N_HEADS = 8
A_LORA_W, A_LORA_A, A_LORA_G = 64, 64, 128
A_GN_EPS = HEAD_DIM * 1e-5
IDX_HEADS, IDX_DIM = 4, 64
TOPK_MAX = 256
Q_BLOCK = 128
LEFT_CHUNKS = 8
REL_CLIP = 256
D_KEY_DIM = 32
RET_GN_EPS = 1e-5
D_FF = 5632
CONV_W = 3
LN_EPS = 1e-5
DEEPNORM_ALPHA = (2 * DEPTH) ** 0.25

A_COLS = 3 * GROUP_W + A_LORA_W + A_LORA_A + A_LORA_G
B_COLS = GROUP_W + 2 * HEAD_DIM + IDX_HEADS * IDX_DIM + IDX_DIM + IDX_HEADS
C_COLS = 3 * GROUP_W
D_COLS = 2 * N_HEADS * D_KEY_DIM + 2 * GROUP_W

LANES = 128
SUBLANES = 8
MXU_WIDTH = 256
VMEM_LIMIT = 56 * 1024 * 1024

P_COLS = 6144
PC_Q, PC_K, PC_V = 0, 512, 1024
PD_Q, PD_K, PD_V, PD_G = 1536, 1792, 2048, 2560
PB_Q, PB_QI, PB_KV = 3072, 3584, 3840
PA = 4096
INT_MIN = -2 ** 31


def _cparams(sem):
    return pltpu.CompilerParams(dimension_semantics=sem, vmem_limit_bytes=VMEM_LIMIT)


def _sigmoid(x):
    return 1.0 / (1.0 + jnp.exp(-x))


def _silu(x):
    return x * _sigmoid(x)


def _dot(a, b, **kw):
    return jnp.dot(a, b, preferred_element_type=F32, **kw)


def _dot_nt(a, b, **kw):
    return lax.dot_general(a, b, (((1,), (1,)), ((), ())), preferred_element_type=F32, **kw)


def _mod_kernel(c_ref, w_ref, b_ref, o_ref):
    ca = _silu(c_ref[...])
    o_ref[0] = _dot(ca, w_ref[0], precision=HIGHEST) + b_ref[0]


def _modulation(c, w_mod, b_mod):
    depth, d, n6 = w_mod.shape
    bsz = c.shape[0]
    tn = 1024
    return pl.pallas_call(
        _mod_kernel,
        out_shape=jax.ShapeDtypeStruct((depth, bsz, n6), F32),
        grid=(depth, n6 // tn),
        in_specs=[
            pl.BlockSpec((bsz, d), lambda l, j: (0, 0)),
            pl.BlockSpec((1, d, tn), lambda l, j: (l, 0, j)),
            pl.BlockSpec((1, 1, tn), lambda l, j: (l, 0, j)),
        ],
        out_specs=pl.BlockSpec((1, bsz, tn), lambda l, j: (l, 0, j)),
        compiler_params=_cparams(("parallel", "parallel")),
        name="adaln_mod",
    )(c, w_mod, b_mod.reshape(depth, 1, n6))


def _inproj_kernel(x_ref, sc_ref, sh_ref, w_ref, o_ref, h_scr):
    @pl.when(pl.program_id(2) == 0)
    def _():
        h_scr[...] = (x_ref[0] * (1.0 + sc_ref[0]) + sh_ref[0]).astype(BF16)

    o_ref[0] = _dot(h_scr[...], w_ref[...])


def _in_projection(x, sc, sh, w):
    bsz, t_len, d = x.shape
    n = w.shape[1]
    tm = min(1024, t_len)
    tn = 512
    return pl.pallas_call(
        _inproj_kernel,
        out_shape=jax.ShapeDtypeStruct((bsz, t_len, n), F32),
        grid=(bsz, t_len // tm, n // tn),
        in_specs=[
            pl.BlockSpec((1, tm, d), lambda b, i, j: (b, i, 0)),
            pl.BlockSpec((1, 1, d), lambda b, i, j: (b, 0, 0)),
            pl.BlockSpec((1, 1, d), lambda b, i, j: (b, 0, 0)),
            pl.BlockSpec((d, tn), lambda b, i, j: (0, j)),
        ],
        out_specs=pl.BlockSpec((1, tm, tn), lambda b, i, j: (b, i, j)),
        scratch_shapes=[pltpu.VMEM((tm, d), BF16)],
        compiler_params=_cparams(("parallel", "parallel", "arbitrary")),
        name="in_proj",
    )(x, sc, sh, w)


def _head_segment_ones():
    seg = (np.arange(GROUP_W)[:, None] // HEAD_DIM == np.arange(GROUP_W)[None, :] // HEAD_DIM)
    return jnp.asarray(seg, F32)


def _rwkv_prep_kernel(p_ref, prev_ref, mu_ref, w0_ref, w2_ref, a0_ref, a2_ref, g2_ref,
                      kk_ref, ka_ref, seg_ref,
                      r_out, w_out, k_out, v_out, a_out, b_out, g_out):
    i = pl.program_id(1)
    p = p_ref[0][:, :A_COLS]
    last_prev = prev_ref[0][SUBLANES - 1:SUBLANES, :A_COLS]
    last_prev = jnp.where(i == 0, 0.0, last_prev)
    rows = lax.broadcasted_iota(I32, p.shape, 0)
    shifted = jnp.where(rows == 0, last_prev, pltpu.roll(p, 1, 0))
    xs = p + (shifted - p) * mu_ref[...]
    r = xs[:, 0:512]
    k = xs[:, 512:1024]
    v = xs[:, 1024:1536]
    wl = xs[:, 1536:1600]
    al = xs[:, 1600:1664]
    gl = xs[:, 1664:1792]
    z = w0_ref[...] + _dot(jnp.tanh(wl), w2_ref[...], precision=HIGHEST)
    nz = -z
    softplus = jnp.maximum(nz, 0.0) + jnp.log1p(jnp.exp(-jnp.abs(nz)))
    w_log = -softplus - 0.5
    a = _sigmoid(a0_ref[...] + _dot(al, a2_ref[...], precision=HIGHEST))
    kk = k * kk_ref[...]
    ssq = _dot(kk * kk, seg_ref[...], precision=HIGHEST)
    kk = kk / jnp.maximum(jnp.sqrt(ssq), 1e-12)
    r_out[0] = r
    w_out[0] = jnp.exp(-jnp.exp(w_log))
    k_out[0] = k * (1.0 + (a - 1.0) * ka_ref[...])
    v_out[0] = v
    a_out[0] = -kk
    b_out[0] = kk * a
    g_out[0] = _dot(_sigmoid(gl), g2_ref[...], precision=HIGHEST)


def _rwkv_prep(proj, mu, w0, w2, a0, a2, g2, k_k, k_a):
    bsz, t_len, _ = proj.shape
    tb = min(256, t_len)
    a_blk = PA // 2048
    row = lambda z: z.reshape(1, -1)
    full = lambda shp: pl.BlockSpec(shp, lambda b, i: (0,) * len(shp))
    out = jax.ShapeDtypeStruct((bsz, t_len, GROUP_W), F32)
    out_spec = pl.BlockSpec((1, tb, GROUP_W), lambda b, i: (b, i, 0))
    return pl.pallas_call(
        _rwkv_prep_kernel,
        out_shape=[out] * 7,
        grid=(bsz, t_len // tb),
        in_specs=[
            pl.BlockSpec((1, tb, 2048), lambda b, i: (b, i, a_blk)),
            pl.BlockSpec((1, SUBLANES, 2048),
                         lambda b, i: (b, jnp.maximum(i * (tb // SUBLANES) - 1, 0), a_blk)),
            full((1, A_COLS)), full((1, GROUP_W)), full((A_LORA_W, GROUP_W)),
            full((1, GROUP_W)), full((A_LORA_A, GROUP_W)), full((A_LORA_G, GROUP_W)),
            full((1, GROUP_W)), full((1, GROUP_W)), full((GROUP_W, GROUP_W)),
        ],
        out_specs=[out_spec] * 7,
        compiler_params=_cparams(("parallel", "parallel")),
        name="rwkv_prep",
    )(proj, proj, row(mu), row(w0), w2, row(a0), a2, g2, row(k_k), row(k_a), _head_segment_ones())


SCAN_CHAINS = LANES // 2
SCAN_KH = HEAD_DIM // 2
SCAN_TB = 64


def _key_sum(x):
    s = jnp.sum(x, axis=0, keepdims=True)
    return s + pltpu.roll(s, SCAN_CHAINS, 1)


def _rwkv_scan_kernel(r_ref, w_ref, k_ref, v_ref, a_ref, b_ref, y_ref, s_scr):
    @pl.when(pl.program_id(1) == 0)
    def _():
        s_scr[...] = jnp.zeros_like(s_scr)

    tb = r_ref.shape[1]

    def step(t, carry):
        r_t, w_t, k_t, a_t, b_t = (ref[0, t] for ref in (r_ref, w_ref, k_ref, a_ref, b_ref))
        for v in range(HEAD_DIM):
            s = s_scr[v]
            sa = _key_sum(s * a_t)
            s = s * w_t + sa * b_t + v_ref[0, t, v:v + 1, :] * k_t
            s_scr[v] = s
            y_ref[0, t, v:v + 1, :] = _key_sum(s * r_t)
        return carry

    lax.fori_loop(0, tb, step, 0)


def _to_scan_layout(z, key_split):
    bsz, t_len, _ = z.shape
    pad = (-bsz) % (SCAN_CHAINS // N_HEADS)
    if pad:
        z = jnp.pad(z, ((0, pad), (0, 0), (0, 0)))
    g = z.shape[0] * N_HEADS // SCAN_CHAINS
    bg = SCAN_CHAINS // N_HEADS
    if key_split:
        z = z.reshape(g, bg, t_len, N_HEADS, 2, SCAN_KH)
        z = jnp.transpose(z, (0, 2, 5, 4, 1, 3))
        return z.reshape(g, t_len, SCAN_KH, LANES)
    z = z.reshape(g, bg, t_len, N_HEADS, HEAD_DIM)
    z = jnp.transpose(z, (0, 2, 4, 1, 3)).reshape(g, t_len, HEAD_DIM, SCAN_CHAINS)
    return jnp.concatenate([z, z], axis=-1)


def _from_scan_layout(y, bsz):
    g, t_len = y.shape[:2]
    bg = SCAN_CHAINS // N_HEADS
    y = y[..., :SCAN_CHAINS].reshape(g, t_len, HEAD_DIM, bg, N_HEADS)
    y = jnp.transpose(y, (0, 3, 1, 4, 2)).reshape(g * bg, t_len, GROUP_W)
    return y[:bsz]


def _rwkv_scan(r, w, k, v, a, b):
    bsz, t_len, _ = r.shape
    rs, ws, ks, as_, bs = (_to_scan_layout(z, True) for z in (r, w, k, a, b))
    vs = _to_scan_layout(v, False)
    g = rs.shape[0]
    tb = min(SCAN_TB, t_len)
    kspec = pl.BlockSpec((1, tb, SCAN_KH, LANES), lambda gi, i: (gi, i, 0, 0))
    vspec = pl.BlockSpec((1, tb, HEAD_DIM, LANES), lambda gi, i: (gi, i, 0, 0))
    y = pl.pallas_call(
        _rwkv_scan_kernel,
        out_shape=jax.ShapeDtypeStruct((g, t_len, HEAD_DIM, LANES), F32),
        grid=(g, t_len // tb),
        in_specs=[kspec, kspec, kspec, vspec, kspec, kspec],
        out_specs=vspec,
        scratch_shapes=[pltpu.VMEM((HEAD_DIM, SCAN_KH, LANES), F32)],
        compiler_params=_cparams(("parallel", "arbitrary")),
        name="rwkv_scan",
    )(rs, ws, ks, vs, as_, bs)
    return _from_scan_layout(y, bsz)


def _rwkv_post_kernel(y_ref, r_ref, k_ref, v_ref, g_ref, lng_ref, lnb_ref, rk_ref, seg_ref, o_ref):
    seg = seg_ref[...]
    y = y_ref[0]
    mu = _dot(y, seg, precision=HIGHEST) * (1.0 / HEAD_DIM)
    d = y - mu
    var = _dot(d * d, seg, precision=HIGHEST) * (1.0 / HEAD_DIM)
    yn = d * lax.rsqrt(var + A_GN_EPS) * lng_ref[...] + lnb_ref[...]
    bonus = _dot(r_ref[0] * k_ref[0] * rk_ref[...], seg, precision=HIGHEST) * v_ref[0]
    o_ref[0] = (yn + bonus) * g_ref[0]


def _rwkv_post(y, r, k, v, g, ln_g, ln_b, r_k):
    bsz, t_len, _ = y.shape
    tb = min(256, t_len)
    spec = pl.BlockSpec((1, tb, GROUP_W), lambda b, i: (b, i, 0))
    par = pl.BlockSpec((1, GROUP_W), lambda b, i: (0, 0))
    row = lambda z: z.reshape(1, GROUP_W)
    return pl.pallas_call(
        _rwkv_post_kernel,
        out_shape=jax.ShapeDtypeStruct((bsz, t_len, GROUP_W), F32),
        grid=(bsz, t_len // tb),
        in_specs=[spec] * 5 + [par, par, par, pl.BlockSpec((GROUP_W, GROUP_W), lambda b, i: (0, 0))],
        out_specs=spec,
        compiler_params=_cparams(("parallel", "parallel")),
        name="rwkv_post",
    )(y, r, k, v, g, row(ln_g), row(ln_b), row(r_k), _head_segment_ones())


DSA_KC = 512
NEG_BIG = -1e30


def _split_bf16(x):
    hi = x.astype(BF16)
    lo = (x - hi.astype(F32)).astype(BF16)
    return hi, lo


def _dsa_kernel(q_ref, qi_ref, kv_ref, o_ref, key_scr, bias_scr, *, topk):
    n = pl.program_id(1)
    t_len = kv_ref.shape[1]
    nq = Q_BLOCK
    kc = min(DSA_KC, t_len)
    n_chunks = (n * nq + nq + kc - 1) // kc
    t_row = n * nq + lax.broadcasted_iota(I32, (nq, 1), 0)
    limit = (t_row // CHUNK + 1) * CHUNK

    qi = qi_ref[0]
    qi_rows = jnp.concatenate([qi[:, h * IDX_DIM:(h + 1) * IDX_DIM] for h in range(IDX_HEADS)], axis=0)
    qi_hi, qi_lo = _split_bf16(qi_rows)
    qi3 = jnp.concatenate([qi_hi, qi_hi, qi_lo], axis=-1)
    wi = kv_ref[0, pl.ds(pl.multiple_of(n * nq, nq), nq), 3 * HEAD_DIM:3 * HEAD_DIM + IDX_HEADS]
    wi = wi * (IDX_HEADS ** -0.5 * IDX_DIM ** -0.5)

    def score_chunk(c, carry):
        k0 = pl.multiple_of(c * kc, kc)
        ki_hi, ki_lo = _split_bf16(kv_ref[0, pl.ds(k0, kc), 2 * HEAD_DIM:3 * HEAD_DIM])
        lg = _dot_nt(qi3, jnp.concatenate([ki_hi, ki_lo, ki_hi], axis=-1))
        score = None
        for h in range(IDX_HEADS):
            term = jnp.maximum(lg[h * nq:(h + 1) * nq], 0.0) * wi[:, h:h + 1]
            score = term if score is None else score + term
        score = jnp.where(score == 0.0, 0.0, score)
        bits = pltpu.bitcast(score, I32)
        okey = jnp.where(bits >= 0, bits, bits ^ jnp.int32(0x7FFFFFFF))
        pos = k0 + lax.broadcasted_iota(I32, (nq, kc), 1)
        key_scr[c] = jnp.where(pos < limit, okey, jnp.int32(INT_MIN))
        return carry

    lax.fori_loop(0, n_chunks, score_chunk, 0)

    def count_ge(cand):
        def body(c, acc):
            keys = key_scr[c]
            for s in range(kc // LANES):
                acc = acc + jnp.where(keys[:, s * LANES:(s + 1) * LANES] >= cand, 1.0, 0.0)
            return acc
        acc = lax.fori_loop(0, n_chunks, body, jnp.zeros((nq, LANES), F32))
        return jnp.broadcast_to(jnp.sum(acc, axis=-1, keepdims=True), (nq, LANES))

    kf = jnp.float32(topk)
    m0 = jnp.where(count_ge(jnp.zeros((nq, LANES), I32)) >= kf, jnp.int32(0), jnp.int32(INT_MIN))

    def bit_step(i, m):
        cand = m | (jnp.int32(1) << (jnp.int32(30) - i))
        return jnp.where(count_ge(cand) >= kf, cand, m)

    thr = lax.fori_loop(0, 31, bit_step, m0)
    cnt_gt = count_ge(thr + 1)
    need = kf - cnt_gt

    tri = (lax.broadcasted_iota(I32, (LANES, LANES), 0)
           <= lax.broadcasted_iota(I32, (LANES, LANES), 1)).astype(BF16)

    def select_chunk(c, running):
        keys = key_scr[c]
        parts = []
        for s in range(kc // LANES):
            kt = keys[:, s * LANES:(s + 1) * LANES]
            tie = kt == thr
            prefix = _dot(jnp.where(tie, 1.0, 0.0).astype(BF16), tri)
            rank = running + prefix
            sel = (kt > thr) | (tie & (rank <= need) & (kt != jnp.int32(INT_MIN)))
            parts.append(jnp.where(sel, 0.0, -jnp.inf))
            running = running + jnp.broadcast_to(prefix[:, LANES - 1:LANES], (nq, LANES))
        bias_scr[c] = jnp.concatenate(parts, axis=-1)
        return running

    lax.fori_loop(0, n_chunks, select_chunk, jnp.zeros((nq, LANES), F32))

    q = q_ref[0] * (HEAD_DIM ** -0.5)
    q_rows = jnp.concatenate([q[:, h * HEAD_DIM:(h + 1) * HEAD_DIM] for h in range(N_HEADS)],
                             axis=0).astype(BF16)
    rows = N_HEADS * nq
    head = lax.broadcasted_iota(I32, (rows, 1), 0) // nq
    slope = pltpu.bitcast((126 - head) << 23, F32)

    def attn_chunk(c, carry):
        m, l, acc = carry
        k0 = pl.multiple_of(c * kc, kc)
        kk = kv_ref[0, pl.ds(k0, kc), 0:HEAD_DIM].astype(BF16)
        vv = kv_ref[0, pl.ds(k0, kc), HEAD_DIM:2 * HEAD_DIM].astype(BF16)
        s = _dot_nt(q_rows, kk)
        pos = k0 + lax.broadcasted_iota(I32, (nq, kc), 1)
        dist = jnp.concatenate([jnp.abs(t_row - pos).astype(F32)] * N_HEADS, axis=0)
        bias = jnp.concatenate([bias_scr[c]] * N_HEADS, axis=0)
        s = s - slope * dist + bias
        m_new = jnp.maximum(m, jnp.max(s, axis=-1, keepdims=True))
        alpha = jnp.exp(m - m_new)
        p = jnp.exp(s - m_new)
        l = alpha * l + jnp.sum(p, axis=-1, keepdims=True)
        acc = alpha * acc + _dot(p.astype(BF16), vv)
        return m_new, l, acc

    _, l, acc = lax.fori_loop(0, n_chunks, attn_chunk,
                              (jnp.full((rows, 1), NEG_BIG, F32), jnp.zeros((rows, 1), F32),
                               jnp.zeros((rows, HEAD_DIM), F32)))
    out = acc / l
    o_ref[0] = jnp.concatenate([out[h * nq:(h + 1) * nq] for h in range(N_HEADS)], axis=-1)


def _dsa_mix(proj):
    bsz, t_len, _ = proj.shape
    topk = min(TOPK_MAX, t_len // 4)
    kc = min(DSA_KC, t_len)
    nck = t_len // kc
    return pl.pallas_call(
        functools.partial(_dsa_kernel, topk=topk),
        out_shape=jax.ShapeDtypeStruct((bsz, t_len, GROUP_W), F32),
        grid=(bsz, t_len // Q_BLOCK),
        in_specs=[
            pl.BlockSpec((1, Q_BLOCK, 512), lambda b, n: (b, n, PB_Q // 512)),
            pl.BlockSpec((1, Q_BLOCK, 256), lambda b, n: (b, n, PB_QI // 256)),
            pl.BlockSpec((1, t_len, 256), lambda b, n: (b, 0, PB_KV // 256)),
        ],
        out_specs=pl.BlockSpec((1, Q_BLOCK, GROUP_W), lambda b, n: (b, n, 0)),
        scratch_shapes=[pltpu.VMEM((nck, Q_BLOCK, kc), I32),
                        pltpu.VMEM((nck, Q_BLOCK, kc), F32)],
        compiler_params=_cparams(("parallel", "arbitrary")),
        name="dsa_mix",
    )(proj, proj, proj)


BAND_TQ = LEFT_CHUNKS * CHUNK
BAND_W = (LEFT_CHUNKS + 1) * CHUNK


def _band_kernel(q_ref, kp_ref, kc_ref, vp_ref, vc_ref, bias_ref, o_ref):
    i = pl.program_id(1)
    tq = q_ref.shape[1]
    kwin = jnp.concatenate([kp_ref[0], kc_ref[0]], axis=0).astype(BF16)
    vwin = jnp.concatenate([vp_ref[0], vc_ref[0]], axis=0).astype(BF16)
    col = lax.broadcasted_iota(I32, (CHUNK, BAND_W), 1)
    for c in range(tq // CHUNK):
        r0 = c * CHUNK
        qc = q_ref[0, r0:r0 + CHUNK, :].astype(BF16)
        kc_ = kwin[r0:r0 + BAND_W]
        vc_ = vwin[r0:r0 + BAND_W]
        valid = jnp.logical_or(i > 0, col >= tq - r0)
        heads = [slice(h * HEAD_DIM, (h + 1) * HEAD_DIM) for h in range(N_HEADS)]
        s = jnp.concatenate([_dot_nt(qc[:, sl], kc_[:, sl]) for sl in heads], axis=0)
        s = s * (HEAD_DIM ** -0.5) + bias_ref[...]
        s = jnp.where(jnp.concatenate([valid] * N_HEADS, axis=0), s, -jnp.inf)
        mx = jnp.max(s, axis=-1, keepdims=True)
        p = jnp.exp(s - mx)
        inv_l = 1.0 / jnp.sum(p, axis=-1, keepdims=True)
        p = p.astype(BF16)
        outs = [_dot(p[h * CHUNK:(h + 1) * CHUNK], vc_[:, sl]) * inv_l[h * CHUNK:(h + 1) * CHUNK]
                for h, sl in enumerate(heads)]
        o_ref[0, r0:r0 + CHUNK, :] = jnp.concatenate(outs, axis=-1)


def _band_mix(proj, rel_bias):
    bsz, t_len, _ = proj.shape
    tq = BAND_TQ
    e = (BAND_W - 1) - np.arange(BAND_W + CHUNK - 1)
    seq = rel_bias[:, np.clip(e, -REL_CLIP, REL_CLIP) + REL_CLIP].astype(F32)
    bias = jnp.stack([seq[:, CHUNK - 1 - i:CHUNK - 1 - i + BAND_W] for i in range(CHUNK)], axis=1)
    bias = bias.reshape(N_HEADS * CHUNK, BAND_W)
    cur = lambda blk: pl.BlockSpec((1, tq, 512), lambda b, i: (b, i, blk))
    prev = lambda blk: pl.BlockSpec((1, tq, 512), lambda b, i: (b, jnp.maximum(i - 1, 0), blk))
    return pl.pallas_call(
        _band_kernel,
        out_shape=jax.ShapeDtypeStruct((bsz, t_len, GROUP_W), F32),
        grid=(bsz, t_len // tq),
        in_specs=[cur(PC_Q // 512), prev(PC_K // 512), cur(PC_K // 512),
                  prev(PC_V // 512), cur(PC_V // 512),
                  pl.BlockSpec((N_HEADS * CHUNK, BAND_W), lambda b, i: (0, 0))],
        out_specs=pl.BlockSpec((1, tq, GROUP_W), lambda b, i: (b, i, 0)),
        compiler_params=_cparams(("parallel", "parallel")),
        name="band_attn",
    )(proj, proj, proj, proj, proj, bias)


RET_TB = 512


def _ret_kernel(q_ref, k_ref, v_ref, g_ref, intra_ref, kvd_ref, qd_ref, cd_ref, o_ref, st_scr):
    @pl.when(pl.program_id(1) == 0)
    def _():
        st_scr[...] = jnp.zeros_like(st_scr)

    tb = q_ref.shape[1]
    eye = (lax.broadcasted_iota(I32, (D_KEY_DIM, D_KEY_DIM), 0)
           == lax.broadcasted_iota(I32, (D_KEY_DIM, D_KEY_DIM), 1)).astype(BF16)
    for c in range(tb // CHUNK):
        r0 = c * CHUNK
        q = q_ref[0, r0:r0 + CHUNK, :]
        k = k_ref[0, r0:r0 + CHUNK, :] * (D_KEY_DIM ** -0.5)
        v = v_ref[0, r0:r0 + CHUNK, :]
        g = g_ref[0, r0:r0 + CHUNK, :]
        outs = []
        for h in range(N_HEADS):
            ks = slice(h * D_KEY_DIM, (h + 1) * D_KEY_DIM)
            vs = slice(h * HEAD_DIM, (h + 1) * HEAD_DIM)
            qh, kh, vh = q[:, ks], k[:, ks], v[:, vs]
            vb = vh.astype(BF16)
            s = _dot_nt(qh.astype(BF16), kh.astype(BF16)) * intra_ref[h]
            intra = _dot(s.astype(BF16), vb)
            state = st_scr[h]
            inter = _dot((qh * qd_ref[h]).astype(BF16), state.astype(BF16))
            kd_t = _dot_nt(eye, (kh * kvd_ref[h]).astype(BF16)).astype(BF16)
            st_scr[h] = state * cd_ref[h] + _dot(kd_t, vb)
            y = intra + inter
            mu = jnp.mean(y, axis=-1, keepdims=True)
            d = y - mu
            var = jnp.mean(d * d, axis=-1, keepdims=True)
            outs.append(d * lax.rsqrt(var + RET_GN_EPS))
        o_ref[0, r0:r0 + CHUNK, :] = jnp.concatenate(outs, axis=-1) * _silu(g)


def _ret_mix(proj):
    bsz, t_len, _ = proj.shape
    tb = min(RET_TB, t_len)
    log_gamma = jnp.log1p(-(2.0 ** (-5.0 - jnp.arange(N_HEADS, dtype=F32))))
    pos = jnp.arange(CHUNK, dtype=F32)
    diff = pos[:, None] - pos[None, :]
    intra_decay = jnp.where(diff >= 0, jnp.exp(log_gamma[:, None, None] * jnp.maximum(diff, 0.0)), 0.0)
    kv_decay = jnp.exp(log_gamma[:, None] * (CHUNK - 1 - pos)[None, :])[:, :, None]
    q_decay = jnp.exp(log_gamma[:, None] * (pos + 1)[None, :])[:, :, None]
    chunk_decay = jnp.broadcast_to(jnp.exp(log_gamma * CHUNK)[:, None, None], (N_HEADS, 1, HEAD_DIM))
    full = lambda shp: pl.BlockSpec(shp, lambda b, i: (0,) * len(shp))
    return pl.pallas_call(
        _ret_kernel,
        out_shape=jax.ShapeDtypeStruct((bsz, t_len, GROUP_W), F32),
        grid=(bsz, t_len // tb),
        in_specs=[
            pl.BlockSpec((1, tb, 256), lambda b, i: (b, i, PD_Q // 256)),
            pl.BlockSpec((1, tb, 256), lambda b, i: (b, i, PD_K // 256)),
            pl.BlockSpec((1, tb, 512), lambda b, i: (b, i, PD_V // 512)),
            pl.BlockSpec((1, tb, 512), lambda b, i: (b, i, PD_G // 512)),
            full((N_HEADS, CHUNK, CHUNK)), full((N_HEADS, CHUNK, 1)), full((N_HEADS, CHUNK, 1)),
            full((N_HEADS, 1, HEAD_DIM)),
        ],
        out_specs=pl.BlockSpec((1, tb, GROUP_W), lambda b, i: (b, i, 0)),
        scratch_shapes=[pltpu.VMEM((N_HEADS, D_KEY_DIM, HEAD_DIM), F32)],
        compiler_params=_cparams(("parallel", "arbitrary")),
        name="retention",
    )(proj, proj, proj, proj, intra_decay, kv_decay, q_decay, chunk_decay)


def _deepnorm_ln(x, gate, branch, g, b):
    z = DEEPNORM_ALPHA * x + (1.0 + gate) * branch
    mu = jnp.mean(z, axis=-1, keepdims=True)
    d = z - mu
    var = jnp.mean(d * d, axis=-1, keepdims=True)
    return d * lax.rsqrt(var + LN_EPS) * g + b


def _outproj_kernel(ya_ref, yb_ref, yc_ref, yd_ref, w_ref, x_ref, gt_ref, g_ref, b_ref, o_ref):
    acc = None
    for gi, y_ref in enumerate((ya_ref, yb_ref, yc_ref, yd_ref)):
        part = _dot(y_ref[0].astype(BF16), w_ref[gi * GROUP_W:(gi + 1) * GROUP_W, :])
        acc = part if acc is None else acc + part
    o_ref[0] = _deepnorm_ln(x_ref[0], gt_ref[0], acc, g_ref[...], b_ref[...])


def _out_projection(ys, w, x, gt, ln_g, ln_b):
    bsz, t_len, d = x.shape
    tm = min(256, t_len)
    yspec = pl.BlockSpec((1, tm, GROUP_W), lambda b, i: (b, i, 0))
    xspec = pl.BlockSpec((1, tm, d), lambda b, i: (b, i, 0))
    vec = pl.BlockSpec((1, d), lambda b, i: (0, 0))
    return pl.pallas_call(
        _outproj_kernel,
        out_shape=jax.ShapeDtypeStruct(x.shape, F32),
        grid=(bsz, t_len // tm),
        in_specs=[yspec] * 4 + [pl.BlockSpec(w.shape, lambda b, i: (0, 0), pipeline_mode=pl.Buffered(1)),
                                xspec,
                                pl.BlockSpec((1, 1, d), lambda b, i: (b, 0, 0)), vec, vec],
        out_specs=xspec,
        compiler_params=_cparams(("parallel", "parallel")),
        name="out_proj_ln",
    )(*ys, w, x, gt, ln_g.reshape(1, d), ln_b.reshape(1, d))


def _up_kernel(x_ref, sc_ref, sh_ref, wg_ref, wu_ref, cw_ref, cb_ref, o_ref, h_scr, halo_scr):
    i = pl.program_id(1)
    j = pl.program_id(2)

    @pl.when(j == 0)
    def _():
        h_scr[...] = (x_ref[0] * (1.0 + sc_ref[0]) + sh_ref[0]).astype(BF16)

    @pl.when(i == 0)
    def _():
        halo_scr[j] = jnp.zeros(halo_scr.shape[1:], F32)

    h = h_scr[...]
    tm = h.shape[0]
    rows = lax.broadcasted_iota(I32, (tm, MXU_WIDTH), 0)
    for s in range(wg_ref.shape[1] // MXU_WIDTH):
        cols = slice(s * MXU_WIDTH, (s + 1) * MXU_WIDTH)
        gate = _dot(h, wg_ref[:, cols])
        up = _dot(h, wu_ref[:, cols])
        halo = halo_scr[j, :, cols]
        halo_scr[j, :, cols] = gate[tm - SUBLANES:, :]
        g1 = jnp.where(rows == 0, halo[SUBLANES - 1:SUBLANES, :], pltpu.roll(gate, 1, 0))
        g2 = jnp.where(rows == 0, halo[SUBLANES - 2:SUBLANES - 1, :],
                       jnp.where(rows == 1, halo[SUBLANES - 1:SUBLANES, :], pltpu.roll(gate, 2, 0)))
        cw = cw_ref[:, cols]
        conv = cw[0:1, :] * g2 + cw[1:2, :] * g1 + cw[2:3, :] * gate + cb_ref[:, cols]
        o_ref[0, :, cols] = (_silu(conv) * up).astype(BF16)


def _up_projection(x, sc, sh, wg, wu, conv_w, conv_b):
    bsz, t_len, d = x.shape
    ff = wg.shape[1]
    tm = min(512, t_len)
    tn = 512
    return pl.pallas_call(
        _up_kernel,
        out_shape=jax.ShapeDtypeStruct((bsz, t_len, ff), BF16),
        grid=(bsz, t_len // tm, ff // tn),
        in_specs=[
            pl.BlockSpec((1, tm, d), lambda b, i, j: (b, i, 0)),
            pl.BlockSpec((1, 1, d), lambda b, i, j: (b, 0, 0)),
            pl.BlockSpec((1, 1, d), lambda b, i, j: (b, 0, 0)),
            pl.BlockSpec((d, tn), lambda b, i, j: (0, j)),
            pl.BlockSpec((d, tn), lambda b, i, j: (0, j)),
            pl.BlockSpec((CONV_W, tn), lambda b, i, j: (0, j)),
            pl.BlockSpec((1, tn), lambda b, i, j: (0, j)),
        ],
        out_specs=pl.BlockSpec((1, tm, tn), lambda b, i, j: (b, i, j)),
        scratch_shapes=[pltpu.VMEM((tm, d), BF16), pltpu.VMEM((ff // tn, SUBLANES, tn), F32)],
        compiler_params=_cparams(("parallel", "arbitrary", "arbitrary")),
        name="mlp_up_conv",
    )(x, sc, sh, wg, wu, conv_w, conv_b.reshape(1, ff))


def _down_kernel(f_ref, w_ref, x_ref, gt_ref, g_ref, b_ref, o_ref):
    acc = _dot(f_ref[0], w_ref[...])
    o_ref[0] = _deepnorm_ln(x_ref[0], gt_ref[0], acc, g_ref[...], b_ref[...])


def _down_projection(f, w, x, gt, ln_g, ln_b):
    bsz, t_len, d = x.shape
    ff = f.shape[2]
    tm = min(256, t_len)
    xspec = pl.BlockSpec((1, tm, d), lambda b, i: (b, i, 0))
    vec = pl.BlockSpec((1, d), lambda b, i: (0, 0))
    return pl.pallas_call(
        _down_kernel,
        out_shape=jax.ShapeDtypeStruct(x.shape, F32),
        grid=(bsz, t_len // tm),
        in_specs=[
            pl.BlockSpec((1, tm, ff), lambda b, i: (b, i, 0)),
            pl.BlockSpec((ff, d), lambda b, i: (0, 0), pipeline_mode=pl.Buffered(1)),
            xspec,
            pl.BlockSpec((1, 1, d), lambda b, i: (b, 0, 0)),
            vec, vec,
        ],
        out_specs=xspec,
        compiler_params=_cparams(("parallel", "parallel")),
        name="mlp_down_ln",
    )(f, w, x, gt, ln_g.reshape(1, d), ln_b.reshape(1, d))


_SRC_B = A_COLS
_SRC_C = A_COLS + B_COLS
_SRC_D = A_COLS + B_COLS + C_COLS
PACK_SEGMENTS = (
    (PC_Q, _SRC_C, C_COLS),
    (PD_Q, _SRC_D, D_COLS),
    (PB_Q, _SRC_B, GROUP_W),
    (PB_QI, _SRC_B + GROUP_W + 2 * HEAD_DIM, IDX_HEADS * IDX_DIM),
    (PB_KV, _SRC_B + GROUP_W, 2 * HEAD_DIM),
    (PB_KV + 2 * HEAD_DIM, _SRC_B + GROUP_W + 2 * HEAD_DIM + IDX_HEADS * IDX_DIM, IDX_DIM + IDX_HEADS),
    (PA, 0, A_COLS),
)
PACK_ZERO = ((PB_KV + 3 * HEAD_DIM + IDX_HEADS, PA), (PA + A_COLS, P_COLS))


def _pack_kernel(w_ref, o_ref):
    w = w_ref[0]
    for dst, src, width in PACK_SEGMENTS:
        o_ref[0, :, dst:dst + width] = w[:, src:src + width].astype(BF16)
    for lo, hi in PACK_ZERO:
        o_ref[0, :, lo:hi] = jnp.zeros((w.shape[0], hi - lo), BF16)


def _pack_w_in(w_in):
    depth, d, n_in = w_in.shape
    tk = 256
    return pl.pallas_call(
        _pack_kernel,
        out_shape=jax.ShapeDtypeStruct((depth, d, P_COLS), BF16),
        grid=(depth, d // tk),
        in_specs=[pl.BlockSpec((1, tk, n_in), lambda l, i: (l, i, 0))],
        out_specs=pl.BlockSpec((1, tk, P_COLS), lambda l, i: (l, i, 0)),
        compiler_params=_cparams(("parallel", "parallel")),
        name="pack_w_in",
    )(w_in)


def kernel(x, c, w_mod, b_mod, w_in, rwkv_mu, rwkv_w0, rwkv_w2, rwkv_a0, rwkv_a2, rwkv_g2,
           rwkv_kk, rwkv_ka, rwkv_rk, rwkv_ln_g, rwkv_ln_b, chunk_rel_bias, w_out, ln1_g, ln1_b,
           w_up, conv_w, conv_b, w_down, ln2_g, ln2_b):
    depth = w_mod.shape[0]
    d = x.shape[-1]
    mod = _modulation(c, w_mod, b_mod)
    w_in_packed = _pack_w_in(w_in)
    for l in range(depth):
        sh1, sc1, gt1, sh2, sc2, gt2 = [mod[l, :, None, i * d:(i + 1) * d] for i in range(6)]
        proj = _in_projection(x, sc1, sh1, w_in_packed[l])
        r, w, k, v, a, b, g = _rwkv_prep(proj, rwkv_mu[l], rwkv_w0[l], rwkv_w2[l], rwkv_a0[l],
                                         rwkv_a2[l], rwkv_g2[l], rwkv_kk[l], rwkv_ka[l])
        y = _rwkv_scan(r, w, k, v, a, b)
        y_a = _rwkv_post(y, r, k, v, g, rwkv_ln_g[l], rwkv_ln_b[l], rwkv_rk[l])
        y_b = _dsa_mix(proj)
        y_c = _band_mix(proj, chunk_rel_bias[l])
        y_d = _ret_mix(proj)
        x = _out_projection((y_a, y_b, y_c, y_d), w_out[l].astype(BF16), x, gt1, ln1_g[l], ln1_b[l])
        ff = w_down.shape[1]
        f = _up_projection(x, sc2, sh2, w_up[l, :, :ff].astype(BF16), w_up[l, :, ff:].astype(BF16),
                           conv_w[l], conv_b[l])
        x = _down_projection(f, w_down[l].astype(BF16), x, gt2, ln2_g[l], ln2_b[l])
    return x
```

```python
import functools

import numpy as np
import jax
import jax.numpy as jnp
from jax import lax
from jax.experimental import pallas as pl
from jax.experimental.pallas import tpu as pltpu

F32 = jnp.float32
BF16 = jnp.bfloat16
I32 = jnp.int32
HIGHEST = lax.Precision.HIGHEST

D_MODEL = 2048
DEPTH = 4
CHUNK = 64
GROUP_W = 512
HEAD_DIM = 64
N_HEADS = 8
A_LORA_W, A_LORA_A, A_LORA_G = 64, 64, 128
A_GN_EPS = HEAD_DIM * 1e-5
IDX_HEADS, IDX_DIM = 4, 64
TOPK_MAX = 256
Q_BLOCK = 128
LEFT_CHUNKS = 8
REL_CLIP = 256
D_KEY_DIM = 32
RET_GN_EPS = 1e-5
D_FF = 5632
CONV_W = 3
LN_EPS = 1e-5
DEEPNORM_ALPHA = (2 * DEPTH) ** 0.25

A_COLS = 3 * GROUP_W + A_LORA_W + A_LORA_A + A_LORA_G
B_COLS = GROUP_W + 2 * HEAD_DIM + IDX_HEADS * IDX_DIM + IDX_DIM + IDX_HEADS
C_COLS = 3 * GROUP_W
D_COLS = 2 * N_HEADS * D_KEY_DIM + 2 * GROUP_W

LANES = 128
SUBLANES = 8
MXU_WIDTH = 256
VMEM_LIMIT = 56 * 1024 * 1024

P_COLS = 6144
PC_Q, PC_K, PC_V = 0, 512, 1024
PD_Q, PD_K, PD_V, PD_G = 1536, 1792, 2048, 2560
PB_Q, PB_QI, PB_KV = 3072, 3584, 3840
PA = 4096
INT_MIN = -2 ** 31


def _cparams(sem):
    return pltpu.CompilerParams(dimension_semantics=sem, vmem_limit_bytes=VMEM_LIMIT)


def _sigmoid(x):
    return 1.0 / (1.0 + jnp.exp(-x))


def _silu(x):
    return x * _sigmoid(x)


def _dot(a, b, **kw):
    return jnp.dot(a, b, preferred_element_type=F32, **kw)


def _dot_nt(a, b, **kw):
    return lax.dot_general(a, b, (((1,), (1,)), ((), ())), preferred_element_type=F32, **kw)


def _mod_kernel(c_ref, w_ref, b_ref, o_ref):
    ca = _silu(c_ref[...])
    o_ref[0] = _dot(ca, w_ref[0], precision=HIGHEST) + b_ref[0]


def _modulation(c, w_mod, b_mod):
    depth, d, n6 = w_mod.shape
    bsz = c.shape[0]
    tn = 1024
    return pl.pallas_call(
        _mod_kernel,
        out_shape=jax.ShapeDtypeStruct((depth, bsz, n6), F32),
        grid=(depth, n6 // tn),
        in_specs=[
            pl.BlockSpec((bsz, d), lambda l, j: (0, 0)),
            pl.BlockSpec((1, d, tn), lambda l, j: (l, 0, j)),
            pl.BlockSpec((1, 1, tn), lambda l, j: (l, 0, j)),
        ],
        out_specs=pl.BlockSpec((1, bsz, tn), lambda l, j: (l, 0, j)),
        compiler_params=_cparams(("parallel", "parallel")),
        name="adaln_mod",
    )(c, w_mod, b_mod.reshape(depth, 1, n6))


def _inproj_kernel(x_ref, sc_ref, sh_ref, w_ref, o_ref, h_scr):
    @pl.when(pl.program_id(2) == 0)
    def _():
        h_scr[...] = (x_ref[0] * (1.0 + sc_ref[0]) + sh_ref[0]).astype(BF16)

    o_ref[0] = _dot(h_scr[...], w_ref[...])


def _in_projection(x, sc, sh, w):
    bsz, t_len, d = x.shape
    n = w.shape[1]
    tm = min(1024, t_len)
    tn = 512
    return pl.pallas_call(
        _inproj_kernel,
        out_shape=jax.ShapeDtypeStruct((bsz, t_len, n), F32),
        grid=(bsz, t_len // tm, n // tn),
        in_specs=[
            pl.BlockSpec((1, tm, d), lambda b, i, j: (b, i, 0)),
            pl.BlockSpec((1, 1, d), lambda b, i, j: (b, 0, 0)),
            pl.BlockSpec((1, 1, d), lambda b, i, j: (b, 0, 0)),
            pl.BlockSpec((d, tn), lambda b, i, j: (0, j)),
        ],
        out_specs=pl.BlockSpec((1, tm, tn), lambda b, i, j: (b, i, j)),
        scratch_shapes=[pltpu.VMEM((tm, d), BF16)],
        compiler_params=_cparams(("parallel", "parallel", "arbitrary")),
        name="in_proj",
    )(x, sc, sh, w)


def _head_segment_ones():
    seg = (np.arange(GROUP_W)[:, None] // HEAD_DIM == np.arange(GROUP_W)[None, :] // HEAD_DIM)
    return jnp.asarray(np.concatenate([seg] * 3, axis=0), BF16)


def _head_sum(x, seg3):
    hi = x.astype(BF16)
    r1 = x - hi.astype(F32)
    mid = r1.astype(BF16)
    lo = (r1 - mid.astype(F32)).astype(BF16)
    return _dot(jnp.concatenate([hi, mid, lo], axis=-1), seg3)


def _rwkv_prep_kernel(p_ref, prev_ref, mu_ref, w0_ref, w2_ref, a0_ref, a2_ref, g2_ref,
                      kk_ref, ka_ref, seg_ref,
                      r_out, w_out, k_out, v_out, a_out, b_out, g_out):
    i = pl.program_id(1)
    p = p_ref[0][:, :A_COLS]
    last_prev = prev_ref[0][SUBLANES - 1:SUBLANES, :A_COLS]
    last_prev = jnp.where(i == 0, 0.0, last_prev)
    rows = lax.broadcasted_iota(I32, p.shape, 0)
    shifted = jnp.where(rows == 0, last_prev, pltpu.roll(p, 1, 0))
    xs = p + (shifted - p) * mu_ref[...]
    r = xs[:, 0:512]
    k = xs[:, 512:1024]
    v = xs[:, 1024:1536]
    wl = xs[:, 1536:1600]
    al = xs[:, 1600:1664]
    gl = xs[:, 1664:1792]
    z = w0_ref[...] + _dot(jnp.tanh(wl), w2_ref[...], precision=HIGHEST)
    nz = -z
    softplus = jnp.maximum(nz, 0.0) + jnp.log1p(jnp.exp(-jnp.abs(nz)))
    w_log = -softplus - 0.5
    a = _sigmoid(a0_ref[...] + _dot(al, a2_ref[...], precision=HIGHEST))
    kk = k * kk_ref[...]
    ssq = _head_sum(kk * kk, seg_ref[...])
    kk = kk / jnp.maximum(jnp.sqrt(ssq), 1e-12)
    r_out[0] = r
    w_out[0] = jnp.exp(-jnp.exp(w_log))
    k_out[0] = k * (1.0 + (a - 1.0) * ka_ref[...])
    v_out[0] = v
    a_out[0] = -kk
    b_out[0] = kk * a
    g_out[0] = _dot(_sigmoid(gl), g2_ref[...], precision=HIGHEST)


def _rwkv_prep(proj, mu, w0, w2, a0, a2, g2, k_k, k_a):
    bsz, t_len, _ = proj.shape
    tb = min(256, t_len)
    a_blk = PA // 2048
    row = lambda z: z.reshape(1, -1)
    full = lambda shp: pl.BlockSpec(shp, lambda b, i: (0,) * len(shp))
    out = jax.ShapeDtypeStruct((bsz, t_len, GROUP_W), F32)
    out_spec = pl.BlockSpec((1, tb, GROUP_W), lambda b, i: (b, i, 0))
    return pl.pallas_call(
        _rwkv_prep_kernel,
        out_shape=[out] * 7,
        grid=(bsz, t_len // tb),
        in_specs=[
            pl.BlockSpec((1, tb, 2048), lambda b, i: (b, i, a_blk)),
            pl.BlockSpec((1, SUBLANES, 2048),
                         lambda b, i: (b, jnp.maximum(i * (tb // SUBLANES) - 1, 0), a_blk)),
            full((1, A_COLS)), full((1, GROUP_W)), full((A_LORA_W, GROUP_W)),
            full((1, GROUP_W)), full((A_LORA_A, GROUP_W)), full((A_LORA_G, GROUP_W)),
            full((1, GROUP_W)), full((1, GROUP_W)), full((3 * GROUP_W, GROUP_W)),
        ],
        out_specs=[out_spec] * 7,
        compiler_params=_cparams(("parallel", "parallel")),
        name="rwkv_prep",
    )(proj, proj, row(mu), row(w0), w2, row(a0), a2, g2, row(k_k), row(k_a), _head_segment_ones())


SCAN_CHAINS = LANES // 2
SCAN_KH = HEAD_DIM // 2
SCAN_TB = 64


def _fold_halves(x):
    return x + pltpu.roll(x, SCAN_CHAINS, 1)


def _rwkv_scan_kernel(r_ref, w_ref, k_ref, v_ref, a_ref, b_ref, y_ref, s_scr, bc_scr):
    @pl.when(pl.program_id(1) == 0)
    def _():
        s_scr[...] = jnp.zeros_like(s_scr)

    tb = r_ref.shape[1]

    def step(t, carry):
        for idx, ref in enumerate((a_ref, w_ref, b_ref, k_ref, r_ref)):
            x = ref[0, t]
            for kk in range(SCAN_KH):
                bc_scr[idx, kk] = jnp.broadcast_to(x[kk:kk + 1, :], (SUBLANES, LANES))
        for g in range(HEAD_DIM // SUBLANES):
            vv = v_ref[0, t, g * SUBLANES:(g + 1) * SUBLANES, :]
            sa = None
            for kk in range(SCAN_KH):
                term = s_scr[g, kk] * bc_scr[0, kk]
                sa = term if sa is None else sa + term
            sa = _fold_halves(sa)
            y = None
            for kk in range(SCAN_KH):
                s = s_scr[g, kk] * bc_scr[1, kk] + sa * bc_scr[2, kk] + vv * bc_scr[3, kk]
                s_scr[g, kk] = s
                term = s * bc_scr[4, kk]
                y = term if y is None else y + term
            y_ref[0, t, g * SUBLANES:(g + 1) * SUBLANES, :] = _fold_halves(y)
        return carry

    lax.fori_loop(0, tb, step, 0)


def _to_scan_layout(z, key_split):
    bsz, t_len, _ = z.shape
    pad = (-bsz) % (SCAN_CHAINS // N_HEADS)
    if pad:
        z = jnp.pad(z, ((0, pad), (0, 0), (0, 0)))
    g = z.shape[0] * N_HEADS // SCAN_CHAINS
    bg = SCAN_CHAINS // N_HEADS
    if key_split:
        z = z.reshape(g, bg, t_len, N_HEADS, 2, SCAN_KH)
        z = jnp.transpose(z, (0, 2, 5, 4, 1, 3))
        return z.reshape(g, t_len, SCAN_KH, LANES)
    z = z.reshape(g, bg, t_len, N_HEADS, HEAD_DIM)
    z = jnp.transpose(z, (0, 2, 4, 1, 3)).reshape(g, t_len, HEAD_DIM, SCAN_CHAINS)
    return jnp.concatenate([z, z], axis=-1)


def _from_scan_layout(y, bsz):
    g, t_len = y.shape[:2]
    bg = SCAN_CHAINS // N_HEADS
    y = y[..., :SCAN_CHAINS].reshape(g, t_len, HEAD_DIM, bg, N_HEADS)
    y = jnp.transpose(y, (0, 3, 1, 4, 2)).reshape(g * bg, t_len, GROUP_W)
    return y[:bsz]


def _rwkv_scan(r, w, k, v, a, b):
    bsz, t_len, _ = r.shape
    rs, ws, ks, as_, bs = (_to_scan_layout(z, True) for z in (r, w, k, a, b))
    vs = _to_scan_layout(v, False)
    g = rs.shape[0]
    tb = min(SCAN_TB, t_len)
    kspec = pl.BlockSpec((1, tb, SCAN_KH, LANES), lambda gi, i: (gi, i, 0, 0))
    vspec = pl.BlockSpec((1, tb, HEAD_DIM, LANES), lambda gi, i: (gi, i, 0, 0))
    y = pl.pallas_call(
        _rwkv_scan_kernel,
        out_shape=jax.ShapeDtypeStruct((g, t_len, HEAD_DIM, LANES), F32),
        grid=(g, t_len // tb),
        in_specs=[kspec, kspec, kspec, vspec, kspec, kspec],
        out_specs=vspec,
        scratch_shapes=[pltpu.VMEM((HEAD_DIM // SUBLANES, SCAN_KH, SUBLANES, LANES), F32),
                        pltpu.VMEM((5, SCAN_KH, SUBLANES, LANES), F32)],
        compiler_params=_cparams(("parallel", "arbitrary")),
        name="rwkv_scan",
    )(rs, ws, ks, vs, as_, bs)
    return _from_scan_layout(y, bsz)


def _rwkv_post_kernel(y_ref, r_ref, k_ref, v_ref, g_ref, lng_ref, lnb_ref, rk_ref, seg_ref, o_ref):
    seg = seg_ref[...]
    y = y_ref[0]
    mu = _head_sum(y, seg) * (1.0 / HEAD_DIM)
    d = y - mu
    var = _head_sum(d * d, seg) * (1.0 / HEAD_DIM)
    yn = d * lax.rsqrt(var + A_GN_EPS) * lng_ref[...] + lnb_ref[...]
    bonus = _head_sum(r_ref[0] * k_ref[0] * rk_ref[...], seg) * v_ref[0]
    o_ref[0] = (yn + bonus) * g_ref[0]


def _rwkv_post(y, r, k, v, g, ln_g, ln_b, r_k):
    bsz, t_len, _ = y.shape
    tb = min(256, t_len)
    spec = pl.BlockSpec((1, tb, GROUP_W), lambda b, i: (b, i, 0))
    par = pl.BlockSpec((1, GROUP_W), lambda b, i: (0, 0))
    row = lambda z: z.reshape(1, GROUP_W)
    return pl.pallas_call(
        _rwkv_post_kernel,
        out_shape=jax.ShapeDtypeStruct((bsz, t_len, GROUP_W), F32),
        grid=(bsz, t_len // tb),
        in_specs=[spec] * 5 + [par, par, par, pl.BlockSpec((3 * GROUP_W, GROUP_W), lambda b, i: (0, 0))],
        out_specs=spec,
        compiler_params=_cparams(("parallel", "parallel")),
        name="rwkv_post",
    )(y, r, k, v, g, row(ln_g), row(ln_b), row(r_k), _head_segment_ones())


DSA_KC = 512
NEG_BIG = -1e30


def _split_bf16(x):
    hi = x.astype(BF16)
    lo = (x - hi.astype(F32)).astype(BF16)
    return hi, lo


def _dsa_kernel(q_ref, qi_ref, kv_ref, o_ref, key_scr, bias_scr, *, topk):
    n = pl.program_id(1)
    t_len = kv_ref.shape[1]
    nq = Q_BLOCK
    kc = min(DSA_KC, t_len)
    n_chunks = (n * nq + nq + kc - 1) // kc
    t_row = n * nq + lax.broadcasted_iota(I32, (nq, 1), 0)
    limit = (t_row // CHUNK + 1) * CHUNK

    qi = qi_ref[0]
    qi_rows = jnp.concatenate([qi[:, h * IDX_DIM:(h + 1) * IDX_DIM] for h in range(IDX_HEADS)], axis=0)
    qi_hi, qi_lo = _split_bf16(qi_rows)
    qi3 = jnp.concatenate([qi_hi, qi_hi, qi_lo], axis=-1)
    wi = kv_ref[0, pl.ds(pl.multiple_of(n * nq, nq), nq), 3 * HEAD_DIM:3 * HEAD_DIM + IDX_HEADS]
    wi = wi * (IDX_HEADS ** -0.5 * IDX_DIM ** -0.5)

    def score_chunk(c, carry):
        k0 = pl.multiple_of(c * kc, kc)
        ki_hi, ki_lo = _split_bf16(kv_ref[0, pl.ds(k0, kc), 2 * HEAD_DIM:3 * HEAD_DIM])
        lg = _dot_nt(qi3, jnp.concatenate([ki_hi, ki_lo, ki_hi], axis=-1))
        score = None
        for h in range(IDX_HEADS):
            term = jnp.maximum(lg[h * nq:(h + 1) * nq], 0.0) * wi[:, h:h + 1]
            score = term if score is None else score + term
        score = jnp.where(score == 0.0, 0.0, score)
        bits = pltpu.bitcast(score, I32)
        okey = jnp.where(bits >= 0, bits, bits ^ jnp.int32(0x7FFFFFFF))
        pos = k0 + lax.broadcasted_iota(I32, (nq, kc), 1)
        key_scr[c] = jnp.where(pos < limit, okey, jnp.int32(INT_MIN))
        return carry

    lax.fori_loop(0, n_chunks, score_chunk, 0)

    def count_ge(cand):
        def body(c, acc):
            keys = key_scr[c]
            for s in range(kc // LANES):
                acc = acc + jnp.where(keys[:, s * LANES:(s + 1) * LANES] >= cand, 1.0, 0.0)
            return acc
        acc = lax.fori_loop(0, n_chunks, body, jnp.zeros((nq, LANES), F32))
        return jnp.broadcast_to(jnp.sum(acc, axis=-1, keepdims=True), (nq, LANES))

    kf = jnp.float32(topk)
    cnt0 = count_ge(jnp.zeros((nq, LANES), I32))
    m0 = jnp.where(cnt0 >= kf, jnp.int32(0), jnp.int32(INT_MIN))
    c0 = jnp.where(cnt0 >= kf, cnt0, kf)

    def bit_step(i, carry):
        m, cnt_m = carry
        cand = m | (jnp.int32(1) << (jnp.int32(30) - i))
        cnt = count_ge(cand)
        take = cnt >= kf
        return jnp.where(take, cand, m), jnp.where(take, cnt, cnt_m)

    thr, cnt_thr = lax.fori_loop(0, 31, bit_step, (m0, c0))
    n_tied_rows = jnp.sum(jnp.where(cnt_thr == kf, 0.0, 1.0))

    def select_all_ties():
        def body(c, carry):
            keys = key_scr[c]
            parts = []
            for s in range(kc // LANES):
                kt = keys[:, s * LANES:(s + 1) * LANES]
                parts.append(jnp.where((kt >= thr) & (kt != jnp.int32(INT_MIN)), 0.0, -jnp.inf))
            bias_scr[c] = jnp.concatenate(parts, axis=-1)
            return carry
        lax.fori_loop(0, n_chunks, body, 0)

    def select_ranked_ties():
        need = kf - count_ge(thr + 1)
        tri = (lax.broadcasted_iota(I32, (LANES, LANES), 0)
               <= lax.broadcasted_iota(I32, (LANES, LANES), 1)).astype(BF16)

        def body(c, running):
            keys = key_scr[c]
            parts = []
            for s in range(kc // LANES):
                kt = keys[:, s * LANES:(s + 1) * LANES]
                tie = kt == thr
                prefix = _dot(jnp.where(tie, 1.0, 0.0).astype(BF16), tri)
                rank = running + prefix
                sel = (kt > thr) | (tie & (rank <= need) & (kt != jnp.int32(INT_MIN)))
                parts.append(jnp.where(sel, 0.0, -jnp.inf))
                running = running + jnp.broadcast_to(prefix[:, LANES - 1:LANES], (nq, LANES))
            bias_scr[c] = jnp.concatenate(parts, axis=-1)
            return running
        lax.fori_loop(0, n_chunks, body, jnp.zeros((nq, LANES), F32))

    lax.cond(n_tied_rows == 0.0, select_all_ties, select_ranked_ties)

    q = q_ref[0] * (HEAD_DIM ** -0.5)
    q_rows = jnp.concatenate([q[:, h * HEAD_DIM:(h + 1) * HEAD_DIM] for h in range(N_HEADS)],
                             axis=0).astype(BF16)
    rows = N_HEADS * nq
    head = lax.broadcasted_iota(I32, (rows, 1), 0) // nq
    slope = pltpu.bitcast((126 - head) << 23, F32)

    def attn_chunk(c, carry):
        m, l, acc = carry
        k0 = pl.multiple_of(c * kc, kc)
        kk = kv_ref[0, pl.ds(k0, kc), 0:HEAD_DIM].astype(BF16)
        vv = kv_ref[0, pl.ds(k0, kc), HEAD_DIM:2 * HEAD_DIM].astype(BF16)
        s = _dot_nt(q_rows, kk)
        pos = k0 + lax.broadcasted_iota(I32, (nq, kc), 1)
        dist = jnp.concatenate([jnp.abs(t_row - pos).astype(F32)] * N_HEADS, axis=0)
        bias = jnp.concatenate([bias_scr[c]] * N_HEADS, axis=0)
        s = s - slope * dist + bias
        m_new = jnp.maximum(m, jnp.max(s, axis=-1, keepdims=True))
        alpha = jnp.exp(m - m_new)
        p = jnp.exp(s - m_new)
        l = alpha * l + jnp.sum(p, axis=-1, keepdims=True)
        acc = alpha * acc + _dot(p.astype(BF16), vv)
        return m_new, l, acc

    _, l, acc = lax.fori_loop(0, n_chunks, attn_chunk,
                              (jnp.full((rows, 1), NEG_BIG, F32), jnp.zeros((rows, 1), F32),
                               jnp.zeros((rows, HEAD_DIM), F32)))
    out = acc / l
    o_ref[0] = jnp.concatenate([out[h * nq:(h + 1) * nq] for h in range(N_HEADS)], axis=-1)


def _dsa_mix(proj):
    bsz, t_len, _ = proj.shape
    topk = min(TOPK_MAX, t_len // 4)
    kc = min(DSA_KC, t_len)
    nck = t_len // kc
    return pl.pallas_call(
        functools.partial(_dsa_kernel, topk=topk),
        out_shape=jax.ShapeDtypeStruct((bsz, t_len, GROUP_W), F32),
        grid=(bsz, t_len // Q_BLOCK),
        in_specs=[
            pl.BlockSpec((1, Q_BLOCK, 512), lambda b, n: (b, n, PB_Q // 512)),
            pl.BlockSpec((1, Q_BLOCK, 256), lambda b, n: (b, n, PB_QI // 256)),
            pl.BlockSpec((1, t_len, 256), lambda b, n: (b, 0, PB_KV // 256)),
        ],
        out_specs=pl.BlockSpec((1, Q_BLOCK, GROUP_W), lambda b, n: (b, n, 0)),
        scratch_shapes=[pltpu.VMEM((nck, Q_BLOCK, kc), I32),
                        pltpu.VMEM((nck, Q_BLOCK, kc), F32)],
        compiler_params=_cparams(("parallel", "arbitrary")),
        name="dsa_mix",
    )(proj, proj, proj)


BAND_TQ = LEFT_CHUNKS * CHUNK
BAND_W = (LEFT_CHUNKS + 1) * CHUNK


def _band_kernel(q_ref, kp_ref, kc_ref, vp_ref, vc_ref, bias_ref, o_ref):
    i = pl.program_id(1)
    tq = q_ref.shape[1]
    kwin = jnp.concatenate([kp_ref[0], kc_ref[0]], axis=0).astype(BF16)
    vwin = jnp.concatenate([vp_ref[0], vc_ref[0]], axis=0).astype(BF16)
    col = lax.broadcasted_iota(I32, (CHUNK, BAND_W), 1)
    for c in range(tq // CHUNK):
        r0 = c * CHUNK
        qc = q_ref[0, r0:r0 + CHUNK, :].astype(BF16)
        kc_ = kwin[r0:r0 + BAND_W]
        vc_ = vwin[r0:r0 + BAND_W]
        valid = jnp.logical_or(i > 0, col >= tq - r0)
        heads = [slice(h * HEAD_DIM, (h + 1) * HEAD_DIM) for h in range(N_HEADS)]
        s = jnp.concatenate([_dot_nt(qc[:, sl], kc_[:, sl]) for sl in heads], axis=0)
        s = s * (HEAD_DIM ** -0.5) + bias_ref[...]
        s = jnp.where(jnp.concatenate([valid] * N_HEADS, axis=0), s, -jnp.inf)
        mx = jnp.max(s, axis=-1, keepdims=True)
        p = jnp.exp(s - mx)
        inv_l = 1.0 / jnp.sum(p, axis=-1, keepdims=True)
        p = p.astype(BF16)
        outs = [_dot(p[h * CHUNK:(h + 1) * CHUNK], vc_[:, sl]) * inv_l[h * CHUNK:(h + 1) * CHUNK]
                for h, sl in enumerate(heads)]
        o_ref[0, r0:r0 + CHUNK, :] = jnp.concatenate(outs, axis=-1)


def _band_mix(proj, rel_bias):
    bsz, t_len, _ = proj.shape
    tq = BAND_TQ
    e = (BAND_W - 1) - np.arange(BAND_W + CHUNK - 1)
    seq = rel_bias[:, np.clip(e, -REL_CLIP, REL_CLIP) + REL_CLIP].astype(F32)
    bias = jnp.stack([seq[:, CHUNK - 1 - i:CHUNK - 1 - i + BAND_W] for i in range(CHUNK)], axis=1)
    bias = bias.reshape(N_HEADS * CHUNK, BAND_W)
    cur = lambda blk: pl.BlockSpec((1, tq, 512), lambda b, i: (b, i, blk))
    prev = lambda blk: pl.BlockSpec((1, tq, 512), lambda b, i: (b, jnp.maximum(i - 1, 0), blk))
    return pl.pallas_call(
        _band_kernel,
        out_shape=jax.ShapeDtypeStruct((bsz, t_len, GROUP_W), F32),
        grid=(bsz, t_len // tq),
        in_specs=[cur(PC_Q // 512), prev(PC_K // 512), cur(PC_K // 512),
                  prev(PC_V // 512), cur(PC_V // 512),
                  pl.BlockSpec((N_HEADS * CHUNK, BAND_W), lambda b, i: (0, 0))],
        out_specs=pl.BlockSpec((1, tq, GROUP_W), lambda b, i: (b, i, 0)),
        compiler_params=_cparams(("parallel", "parallel")),
        name="band_attn",
    )(proj, proj, proj, proj, proj, bias)


RET_TB = 512


def _ret_kernel(q_ref, k_ref, v_ref, g_ref, intra_ref, kvd_ref, qd_ref, cd_ref, o_ref, st_scr):
    @pl.when(pl.program_id(1) == 0)
    def _():
        st_scr[...] = jnp.zeros_like(st_scr)

    tb = q_ref.shape[1]
    eye = (lax.broadcasted_iota(I32, (D_KEY_DIM, D_KEY_DIM), 0)
           == lax.broadcasted_iota(I32, (D_KEY_DIM, D_KEY_DIM), 1)).astype(BF16)
    for c in range(tb // CHUNK):
        r0 = c * CHUNK
        q = q_ref[0, r0:r0 + CHUNK, :]
        k = k_ref[0, r0:r0 + CHUNK, :] * (D_KEY_DIM ** -0.5)
        v = v_ref[0, r0:r0 + CHUNK, :]
        g = g_ref[0, r0:r0 + CHUNK, :]
        outs = []
        for h in range(N_HEADS):
            ks = slice(h * D_KEY_DIM, (h + 1) * D_KEY_DIM)
            vs = slice(h * HEAD_DIM, (h + 1) * HEAD_DIM)
            qh, kh, vh = q[:, ks], k[:, ks], v[:, vs]
            vb = vh.astype(BF16)
            s = _dot_nt(qh.astype(BF16), kh.astype(BF16)) * intra_ref[h]
            intra = _dot(s.astype(BF16), vb)
            state = st_scr[h]
            inter = _dot((qh * qd_ref[h]).astype(BF16), state.astype(BF16))
            kd_t = _dot_nt(eye, (kh * kvd_ref[h]).astype(BF16)).astype(BF16)
            st_scr[h] = state * cd_ref[h] + _dot(kd_t, vb)
            y = intra + inter
            mu = jnp.mean(y, axis=-1, keepdims=True)
            d = y - mu
            var = jnp.mean(d * d, axis=-1, keepdims=True)
            outs.append(d * lax.rsqrt(var + RET_GN_EPS))
        o_ref[0, r0:r0 + CHUNK, :] = jnp.concatenate(outs, axis=-1) * _silu(g)


def _ret_mix(proj):
    bsz, t_len, _ = proj.shape
    tb = min(RET_TB, t_len)
    log_gamma = jnp.log1p(-(2.0 ** (-5.0 - jnp.arange(N_HEADS, dtype=F32))))
    pos = jnp.arange(CHUNK, dtype=F32)
    diff = pos[:, None] - pos[None, :]
    intra_decay = jnp.where(diff >= 0, jnp.exp(log_gamma[:, None, None] * jnp.maximum(diff, 0.0)), 0.0)
    kv_decay = jnp.exp(log_gamma[:, None] * (CHUNK - 1 - pos)[None, :])[:, :, None]
    q_decay = jnp.exp(log_gamma[:, None] * (pos + 1)[None, :])[:, :, None]
    chunk_decay = jnp.broadcast_to(jnp.exp(log_gamma * CHUNK)[:, None, None], (N_HEADS, 1, HEAD_DIM))
    full = lambda shp: pl.BlockSpec(shp, lambda b, i: (0,) * len(shp))
    return pl.pallas_call(
        _ret_kernel,
        out_shape=jax.ShapeDtypeStruct((bsz, t_len, GROUP_W), F32),
        grid=(bsz, t_len // tb),
        in_specs=[
            pl.BlockSpec((1, tb, 256), lambda b, i: (b, i, PD_Q // 256)),
            pl.BlockSpec((1, tb, 256), lambda b, i: (b, i, PD_K // 256)),
            pl.BlockSpec((1, tb, 512), lambda b, i: (b, i, PD_V // 512)),
            pl.BlockSpec((1, tb, 512), lambda b, i: (b, i, PD_G // 512)),
            full((N_HEADS, CHUNK, CHUNK)), full((N_HEADS, CHUNK, 1)), full((N_HEADS, CHUNK, 1)),
            full((N_HEADS, 1, HEAD_DIM)),
        ],
        out_specs=pl.BlockSpec((1, tb, GROUP_W), lambda b, i: (b, i, 0)),
        scratch_shapes=[pltpu.VMEM((N_HEADS, D_KEY_DIM, HEAD_DIM), F32)],
        compiler_params=_cparams(("parallel", "arbitrary")),
        name="retention",
    )(proj, proj, proj, proj, intra_decay, kv_decay, q_decay, chunk_decay)


def _deepnorm_ln(x, gate, branch, g, b):
    z = DEEPNORM_ALPHA * x + (1.0 + gate) * branch
    mu = jnp.mean(z, axis=-1, keepdims=True)
    d = z - mu
    var = jnp.mean(d * d, axis=-1, keepdims=True)
    return d * lax.rsqrt(var + LN_EPS) * g + b


def _outproj_kernel(ya_ref, yb_ref, yc_ref, yd_ref, w_ref, x_ref, gt_ref, g_ref, b_ref, o_ref):
    acc = None
    for gi, y_ref in enumerate((ya_ref, yb_ref, yc_ref, yd_ref)):
        part = _dot(y_ref[0].astype(BF16), w_ref[gi * GROUP_W:(gi + 1) * GROUP_W, :])
        acc = part if acc is None else acc + part
    o_ref[0] = _deepnorm_ln(x_ref[0], gt_ref[0], acc, g_ref[...], b_ref[...])


def _out_projection(ys, w, x, gt, ln_g, ln_b):
    bsz, t_len, d = x.shape
    tm = min(256, t_len)
    yspec = pl.BlockSpec((1, tm, GROUP_W), lambda b, i: (b, i, 0))
    xspec = pl.BlockSpec((1, tm, d), lambda b, i: (b, i, 0))
    vec = pl.BlockSpec((1, d), lambda b, i: (0, 0))
    return pl.pallas_call(
        _outproj_kernel,
        out_shape=jax.ShapeDtypeStruct(x.shape, F32),
        grid=(bsz, t_len // tm),
        in_specs=[yspec] * 4 + [pl.BlockSpec(w.shape, lambda b, i: (0, 0), pipeline_mode=pl.Buffered(1)),
                                xspec,
                                pl.BlockSpec((1, 1, d), lambda b, i: (b, 0, 0)), vec, vec],
        out_specs=xspec,
        compiler_params=_cparams(("parallel", "parallel")),
        name="out_proj_ln",
    )(*ys, w, x, gt, ln_g.reshape(1, d), ln_b.reshape(1, d))


def _up_kernel(x_ref, sc_ref, sh_ref, wg_ref, wu_ref, cw_ref, cb_ref, o_ref, h_scr, halo_scr):
    i = pl.program_id(1)
    j = pl.program_id(2)

    @pl.when(j == 0)
    def _():
        h_scr[...] = (x_ref[0] * (1.0 + sc_ref[0]) + sh_ref[0]).astype(BF16)

    @pl.when(i == 0)
    def _():
        halo_scr[j] = jnp.zeros(halo_scr.shape[1:], F32)

    h = h_scr[...]
    tm = h.shape[0]
    rows8 = lax.broadcasted_iota(I32, (SUBLANES, MXU_WIDTH), 0)
    for s in range(wg_ref.shape[1] // MXU_WIDTH):
        cols = slice(s * MXU_WIDTH, (s + 1) * MXU_WIDTH)
        gate = _dot(h, wg_ref[:, cols])
        up = _dot(h, wu_ref[:, cols])
        halo = halo_scr[j, :, cols]
        halo_scr[j, :, cols] = gate[tm - SUBLANES:, :]
        r1 = pltpu.roll(gate, 1, 0)
        r2 = pltpu.roll(gate, 2, 0)
        top1 = jnp.where(rows8 == 0, halo[SUBLANES - 1:SUBLANES, :], r1[:SUBLANES])
        top2 = jnp.where(rows8 == 0, halo[SUBLANES - 2:SUBLANES - 1, :],
                         jnp.where(rows8 == 1, halo[SUBLANES - 1:SUBLANES, :], r2[:SUBLANES]))
        g1 = jnp.concatenate([top1, r1[SUBLANES:]], axis=0)
        g2 = jnp.concatenate([top2, r2[SUBLANES:]], axis=0)
        cw = cw_ref[:, cols]
        conv = cw[0:1, :] * g2 + cw[1:2, :] * g1 + cw[2:3, :] * gate + cb_ref[:, cols]
        o_ref[0, :, cols] = (_silu(conv) * up).astype(BF16)


def _up_projection(x, sc, sh, wg, wu, conv_w, conv_b):
    bsz, t_len, d = x.shape
    ff = wg.shape[1]
    tm = min(1024, t_len)
    tn = 512
    return pl.pallas_call(
        _up_kernel,
        out_shape=jax.ShapeDtypeStruct((bsz, t_len, ff), BF16),
        grid=(bsz, t_len // tm, ff // tn),
        in_specs=[
            pl.BlockSpec((1, tm, d), lambda b, i, j: (b, i, 0)),
            pl.BlockSpec((1, 1, d), lambda b, i, j: (b, 0, 0)),
            pl.BlockSpec((1, 1, d), lambda b, i, j: (b, 0, 0)),
            pl.BlockSpec((d, tn), lambda b, i, j: (0, j)),
            pl.BlockSpec((d, tn), lambda b, i, j: (0, j)),
            pl.BlockSpec((CONV_W, tn), lambda b, i, j: (0, j)),
            pl.BlockSpec((1, tn), lambda b, i, j: (0, j)),
        ],
        out_specs=pl.BlockSpec((1, tm, tn), lambda b, i, j: (b, i, j)),
        scratch_shapes=[pltpu.VMEM((tm, d), BF16), pltpu.VMEM((ff // tn, SUBLANES, tn), F32)],
        compiler_params=_cparams(("parallel", "arbitrary", "arbitrary")),
        name="mlp_up_conv",
    )(x, sc, sh, wg, wu, conv_w, conv_b.reshape(1, ff))


def _down_kernel(f_ref, w_ref, x_ref, gt_ref, g_ref, b_ref, o_ref):
    acc = _dot(f_ref[0], w_ref[...])
    o_ref[0] = _deepnorm_ln(x_ref[0], gt_ref[0], acc, g_ref[...], b_ref[...])


def _down_projection(f, w, x, gt, ln_g, ln_b):
    bsz, t_len, d = x.shape
    ff = f.shape[2]
    tm = min(256, t_len)
    xspec = pl.BlockSpec((1, tm, d), lambda b, i: (b, i, 0))
    vec = pl.BlockSpec((1, d), lambda b, i: (0, 0))
    return pl.pallas_call(
        _down_kernel,
        out_shape=jax.ShapeDtypeStruct(x.shape, F32),
        grid=(bsz, t_len // tm),
        in_specs=[
            pl.BlockSpec((1, tm, ff), lambda b, i: (b, i, 0)),
            pl.BlockSpec((ff, d), lambda b, i: (0, 0), pipeline_mode=pl.Buffered(1)),
            xspec,
            pl.BlockSpec((1, 1, d), lambda b, i: (b, 0, 0)),
            vec, vec,
        ],
        out_specs=xspec,
        compiler_params=_cparams(("parallel", "parallel")),
        name="mlp_down_ln",
    )(f, w, x, gt, ln_g.reshape(1, d), ln_b.reshape(1, d))


_SRC_B = A_COLS
_SRC_C = A_COLS + B_COLS
_SRC_D = A_COLS + B_COLS + C_COLS
PACK_SEGMENTS = (
    (PC_Q, _SRC_C, C_COLS),
    (PD_Q, _SRC_D, D_COLS),
    (PB_Q, _SRC_B, GROUP_W),
    (PB_QI, _SRC_B + GROUP_W + 2 * HEAD_DIM, IDX_HEADS * IDX_DIM),
    (PB_KV, _SRC_B + GROUP_W, 2 * HEAD_DIM),
    (PB_KV + 2 * HEAD_DIM, _SRC_B + GROUP_W + 2 * HEAD_DIM + IDX_HEADS * IDX_DIM, IDX_DIM + IDX_HEADS),
    (PA, 0, A_COLS),
)
PACK_ZERO = ((PB_KV + 3 * HEAD_DIM + IDX_HEADS, PA), (PA + A_COLS, P_COLS))


def _pack_kernel(w_ref, o_ref):
    w = w_ref[0]
    for dst, src, width in PACK_SEGMENTS:
        o_ref[0, :, dst:dst + width] = w[:, src:src + width].astype(BF16)
    for lo, hi in PACK_ZERO:
        o_ref[0, :, lo:hi] = jnp.zeros((w.shape[0], hi - lo), BF16)


def _pack_w_in(w_in):
    depth, d, n_in = w_in.shape
    tk = 256
    return pl.pallas_call(
        _pack_kernel,
        out_shape=jax.ShapeDtypeStruct((depth, d, P_COLS), BF16),
        grid=(depth, d // tk),
        in_specs=[pl.BlockSpec((1, tk, n_in), lambda l, i: (l, i, 0))],
        out_specs=pl.BlockSpec((1, tk, P_COLS), lambda l, i: (l, i, 0)),
        compiler_params=_cparams(("parallel", "parallel")),
        name="pack_w_in",
    )(w_in)


def kernel(x, c, w_mod, b_mod, w_in, rwkv_mu, rwkv_w0, rwkv_w2, rwkv_a0, rwkv_a2, rwkv_g2,
           rwkv_kk, rwkv_ka, rwkv_rk, rwkv_ln_g, rwkv_ln_b, chunk_rel_bias, w_out, ln1_g, ln1_b,
           w_up, conv_w, conv_b, w_down, ln2_g, ln2_b):
    depth = w_mod.shape[0]
    d = x.shape[-1]
    mod = _modulation(c, w_mod, b_mod)
    w_in_packed = _pack_w_in(w_in)
    for l in range(depth):
        sh1, sc1, gt1, sh2, sc2, gt2 = [mod[l, :, None, i * d:(i + 1) * d] for i in range(6)]
        proj = _in_projection(x, sc1, sh1, w_in_packed[l])
        r, w, k, v, a, b, g = _rwkv_prep(proj, rwkv_mu[l], rwkv_w0[l], rwkv_w2[l], rwkv_a0[l],
                                         rwkv_a2[l], rwkv_g2[l], rwkv_kk[l], rwkv_ka[l])
        y = _rwkv_scan(r, w, k, v, a, b)
        y_a = _rwkv_post(y, r, k, v, g, rwkv_ln_g[l], rwkv_ln_b[l], rwkv_rk[l])
        y_b = _dsa_mix(proj)
        y_c = _band_mix(proj, chunk_rel_bias[l])
        y_d = _ret_mix(proj)
        x = _out_projection((y_a, y_b, y_c, y_d), w_out[l].astype(BF16), x, gt1, ln1_g[l], ln1_b[l])
        ff = w_down.shape[1]
        f = _up_projection(x, sc2, sh2, w_up[l, :, :ff].astype(BF16), w_up[l, :, ff:].astype(BF16),
                           conv_w[l], conv_b[l])
        x = _down_projection(f, w_down[l].astype(BF16), x, gt2, ln2_g[l], ln2_b[l])
    return x
```

```python
import functools

import numpy as np
import jax
import jax.numpy as jnp
from jax import lax
from jax.experimental import pallas as pl
from jax.experimental.pallas import tpu as pltpu

F32 = jnp.float32
BF16 = jnp.bfloat16
I32 = jnp.int32
HIGHEST = lax.Precision.HIGHEST

D_MODEL = 2048
DEPTH = 4
CHUNK = 64
GROUP_W = 512
HEAD_DIM = 64
N_HEADS = 8
A_LORA_W, A_LORA_A, A_LORA_G = 64, 64, 128
A_GN_EPS = HEAD_DIM * 1e-5
IDX_HEADS, IDX_DIM = 4, 64
TOPK_MAX = 256
Q_BLOCK = 128
LEFT_CHUNKS = 8
REL_CLIP = 256
D_KEY_DIM = 32
RET_GN_EPS = 1e-5
D_FF = 5632
CONV_W = 3
LN_EPS = 1e-5
DEEPNORM_ALPHA = (2 * DEPTH) ** 0.25

A_COLS = 3 * GROUP_W + A_LORA_W + A_LORA_A + A_LORA_G
B_COLS = GROUP_W + 2 * HEAD_DIM + IDX_HEADS * IDX_DIM + IDX_DIM + IDX_HEADS
C_COLS = 3 * GROUP_W
D_COLS = 2 * N_HEADS * D_KEY_DIM + 2 * GROUP_W

LANES = 128
SUBLANES = 8
MXU_WIDTH = 256
VMEM_LIMIT = 56 * 1024 * 1024

P_COLS = 6144
PC_Q, PC_K, PC_V = 0, 512, 1024
PD_Q, PD_K, PD_V, PD_G = 1536, 1792, 2048, 2560
PB_Q, PB_QI, PB_KV = 3072, 3584, 3840
PA = 4096
INT_MIN = -2 ** 31


def _cparams(sem):
    return pltpu.CompilerParams(dimension_semantics=sem, vmem_limit_bytes=VMEM_LIMIT)


def _sigmoid(x):
    return 1.0 / (1.0 + jnp.exp(-x))


def _silu(x):
    return x * _sigmoid(x)


def _dot(a, b, **kw):
    return jnp.dot(a, b, preferred_element_type=F32, **kw)


def _dot_nt(a, b, **kw):
    return lax.dot_general(a, b, (((1,), (1,)), ((), ())), preferred_element_type=F32, **kw)


def _mod_kernel(c_ref, w_ref, b_ref, o_ref):
    ca = _silu(c_ref[...])
    o_ref[0] = _dot(ca, w_ref[0], precision=HIGHEST) + b_ref[0]


def _modulation(c, w_mod, b_mod):
    depth, d, n6 = w_mod.shape
    bsz = c.shape[0]
    tn = 1024
    return pl.pallas_call(
        _mod_kernel,
        out_shape=jax.ShapeDtypeStruct((depth, bsz, n6), F32),
        grid=(depth, n6 // tn),
        in_specs=[
            pl.BlockSpec((bsz, d), lambda l, j: (0, 0)),
            pl.BlockSpec((1, d, tn), lambda l, j: (l, 0, j)),
            pl.BlockSpec((1, 1, tn), lambda l, j: (l, 0, j)),
        ],
        out_specs=pl.BlockSpec((1, bsz, tn), lambda l, j: (l, 0, j)),
        compiler_params=_cparams(("parallel", "parallel")),
        name="adaln_mod",
    )(c, w_mod, b_mod.reshape(depth, 1, n6))


def _inproj_kernel(x_ref, sc_ref, sh_ref, w_ref, o_ref, h_scr):
    @pl.when(pl.program_id(2) == 0)
    def _():
        h_scr[...] = (x_ref[0] * (1.0 + sc_ref[0]) + sh_ref[0]).astype(BF16)

    o_ref[0] = _dot(h_scr[...], w_ref[...])


def _in_projection(x, sc, sh, w):
    bsz, t_len, d = x.shape
    n = w.shape[1]
    tm = min(1024, t_len)
    tn = 512
    return pl.pallas_call(
        _inproj_kernel,
        out_shape=jax.ShapeDtypeStruct((bsz, t_len, n), F32),
        grid=(bsz, t_len // tm, n // tn),
        in_specs=[
            pl.BlockSpec((1, tm, d), lambda b, i, j: (b, i, 0)),
            pl.BlockSpec((1, 1, d), lambda b, i, j: (b, 0, 0)),
            pl.BlockSpec((1, 1, d), lambda b, i, j: (b, 0, 0)),
            pl.BlockSpec((d, tn), lambda b, i, j: (0, j)),
        ],
        out_specs=pl.BlockSpec((1, tm, tn), lambda b, i, j: (b, i, j)),
        scratch_shapes=[pltpu.VMEM((tm, d), BF16)],
        compiler_params=_cparams(("parallel", "parallel", "arbitrary")),
        name="in_proj",
    )(x, sc, sh, w)


def _head_segment_ones():
    seg = (np.arange(GROUP_W)[:, None] // HEAD_DIM == np.arange(GROUP_W)[None, :] // HEAD_DIM)
    return jnp.asarray(np.concatenate([seg] * 3, axis=0), BF16)


def _head_sum(x, seg3):
    hi = x.astype(BF16)
    r1 = x - hi.astype(F32)
    mid = r1.astype(BF16)
    lo = (r1 - mid.astype(F32)).astype(BF16)
    return _dot(jnp.concatenate([hi, mid, lo], axis=-1), seg3)


def _rwkv_prep_kernel(p_ref, prev_ref, mu_ref, w0_ref, w2_ref, a0_ref, a2_ref, g2_ref,
                      kk_ref, ka_ref, seg_ref,
                      r_out, w_out, k_out, v_out, a_out, b_out, g_out):
    i = pl.program_id(1)
    p = p_ref[0][:, :A_COLS]
    last_prev = prev_ref[0][SUBLANES - 1:SUBLANES, :A_COLS]
    last_prev = jnp.where(i == 0, 0.0, last_prev)
    rows = lax.broadcasted_iota(I32, p.shape, 0)
    shifted = jnp.where(rows == 0, last_prev, pltpu.roll(p, 1, 0))
    xs = p + (shifted - p) * mu_ref[...]
    r = xs[:, 0:512]
    k = xs[:, 512:1024]
    v = xs[:, 1024:1536]
    wl = xs[:, 1536:1600]
    al = xs[:, 1600:1664]
    gl = xs[:, 1664:1792]
    z = w0_ref[...] + _dot(jnp.tanh(wl), w2_ref[...], precision=HIGHEST)
    nz = -z
    softplus = jnp.maximum(nz, 0.0) + jnp.log1p(jnp.exp(-jnp.abs(nz)))
    w_log = -softplus - 0.5
    a = _sigmoid(a0_ref[...] + _dot(al, a2_ref[...], precision=HIGHEST))
    kk = k * kk_ref[...]
    ssq = _head_sum(kk * kk, seg_ref[...])
    kk = kk / jnp.maximum(jnp.sqrt(ssq), 1e-12)
    r_out[0] = r
    w_out[0] = jnp.exp(-jnp.exp(w_log))
    k_out[0] = k * (1.0 + (a - 1.0) * ka_ref[...])
    v_out[0] = v
    a_out[0] = -kk
    b_out[0] = kk * a
    g_out[0] = _dot(_sigmoid(gl), g2_ref[...], precision=HIGHEST)


def _rwkv_prep(proj, mu, w0, w2, a0, a2, g2, k_k, k_a):
    bsz, t_len, _ = proj.shape
    tb = min(256, t_len)
    a_blk = PA // 2048
    row = lambda z: z.reshape(1, -1)
    full = lambda shp: pl.BlockSpec(shp, lambda b, i: (0,) * len(shp))
    out = jax.ShapeDtypeStruct((bsz, t_len, GROUP_W), F32)
    out_spec = pl.BlockSpec((1, tb, GROUP_W), lambda b, i: (b, i, 0))
    return pl.pallas_call(
        _rwkv_prep_kernel,
        out_shape=[out] * 7,
        grid=(bsz, t_len // tb),
        in_specs=[
            pl.BlockSpec((1, tb, 2048), lambda b, i: (b, i, a_blk)),
            pl.BlockSpec((1, SUBLANES, 2048),
                         lambda b, i: (b, jnp.maximum(i * (tb // SUBLANES) - 1, 0), a_blk)),
            full((1, A_COLS)), full((1, GROUP_W)), full((A_LORA_W, GROUP_W)),
            full((1, GROUP_W)), full((A_LORA_A, GROUP_W)), full((A_LORA_G, GROUP_W)),
            full((1, GROUP_W)), full((1, GROUP_W)), full((3 * GROUP_W, GROUP_W)),
        ],
        out_specs=[out_spec] * 7,
        compiler_params=_cparams(("parallel", "parallel")),
        name="rwkv_prep",
    )(proj, proj, row(mu), row(w0), w2, row(a0), a2, g2, row(k_k), row(k_a), _head_segment_ones())


SCAN_CHAINS = LANES // 2
SCAN_KH = HEAD_DIM // 2
SCAN_TB = 64


def _fold_halves(x):
    return x + pltpu.roll(x, SCAN_CHAINS, 1)


def _rwkv_scan_kernel(r_ref, w_ref, k_ref, v_ref, a_ref, b_ref, y_ref, s_scr, bc_scr):
    @pl.when(pl.program_id(1) == 0)
    def _():
        s_scr[...] = jnp.zeros_like(s_scr)

    tb = r_ref.shape[1]

    def step(t, carry):
        for idx, ref in enumerate((a_ref, w_ref, b_ref, k_ref, r_ref)):
            x = ref[0, t]
            for kk in range(SCAN_KH):
                bc_scr[idx, kk] = jnp.broadcast_to(x[kk:kk + 1, :], (SUBLANES, LANES))
        for g in range(HEAD_DIM // SUBLANES):
            vv = v_ref[0, t, g * SUBLANES:(g + 1) * SUBLANES, :]
            sa = None
            for kk in range(SCAN_KH):
                term = s_scr[g, kk] * bc_scr[0, kk]
                sa = term if sa is None else sa + term
            sa = _fold_halves(sa)
            y = None
            for kk in range(SCAN_KH):
                s = s_scr[g, kk] * bc_scr[1, kk] + sa * bc_scr[2, kk] + vv * bc_scr[3, kk]
                s_scr[g, kk] = s
                term = s * bc_scr[4, kk]
                y = term if y is None else y + term
            y_ref[0, t, g * SUBLANES:(g + 1) * SUBLANES, :] = _fold_halves(y)
        return carry

    lax.fori_loop(0, tb, step, 0)


def _to_scan_layout(z, key_split):
    bsz, t_len, _ = z.shape
    pad = (-bsz) % (SCAN_CHAINS // N_HEADS)
    if pad:
        z = jnp.pad(z, ((0, pad), (0, 0), (0, 0)))
    g = z.shape[0] * N_HEADS // SCAN_CHAINS
    bg = SCAN_CHAINS // N_HEADS
    if key_split:
        z = z.reshape(g, bg, t_len, N_HEADS, 2, SCAN_KH)
        z = jnp.transpose(z, (0, 2, 5, 4, 1, 3))
        return z.reshape(g, t_len, SCAN_KH, LANES)
    z = z.reshape(g, bg, t_len, N_HEADS, HEAD_DIM)
    z = jnp.transpose(z, (0, 2, 4, 1, 3)).reshape(g, t_len, HEAD_DIM, SCAN_CHAINS)
    return jnp.concatenate([z, z], axis=-1)


def _from_scan_layout(y, bsz):
    g, t_len = y.shape[:2]
    bg = SCAN_CHAINS // N_HEADS
    y = y[..., :SCAN_CHAINS].reshape(g, t_len, HEAD_DIM, bg, N_HEADS)
    y = jnp.transpose(y, (0, 3, 1, 4, 2)).reshape(g * bg, t_len, GROUP_W)
    return y[:bsz]


def _rwkv_scan(r, w, k, v, a, b):
    bsz, t_len, _ = r.shape
    rs, ws, ks, as_, bs = (_to_scan_layout(z, True) for z in (r, w, k, a, b))
    vs = _to_scan_layout(v, False)
    g = rs.shape[0]
    tb = min(SCAN_TB, t_len)
    kspec = pl.BlockSpec((1, tb, SCAN_KH, LANES), lambda gi, i: (gi, i, 0, 0))
    vspec = pl.BlockSpec((1, tb, HEAD_DIM, LANES), lambda gi, i: (gi, i, 0, 0))
    y = pl.pallas_call(
        _rwkv_scan_kernel,
        out_shape=jax.ShapeDtypeStruct((g, t_len, HEAD_DIM, LANES), F32),
        grid=(g, t_len // tb),
        in_specs=[kspec, kspec, kspec, vspec, kspec, kspec],
        out_specs=vspec,
        scratch_shapes=[pltpu.VMEM((HEAD_DIM // SUBLANES, SCAN_KH, SUBLANES, LANES), F32),
                        pltpu.VMEM((5, SCAN_KH, SUBLANES, LANES), F32)],
        compiler_params=_cparams(("parallel", "arbitrary")),
        name="rwkv_scan",
    )(rs, ws, ks, vs, as_, bs)
    return _from_scan_layout(y, bsz)


def _rwkv_post_kernel(y_ref, r_ref, k_ref, v_ref, g_ref, lng_ref, lnb_ref, rk_ref, seg_ref, o_ref):
    seg = seg_ref[...]
    y = y_ref[0]
    mu = _head_sum(y, seg) * (1.0 / HEAD_DIM)
    d = y - mu
    var = _head_sum(d * d, seg) * (1.0 / HEAD_DIM)
    yn = d * lax.rsqrt(var + A_GN_EPS) * lng_ref[...] + lnb_ref[...]
    bonus = _head_sum(r_ref[0] * k_ref[0] * rk_ref[...], seg) * v_ref[0]
    o_ref[0] = (yn + bonus) * g_ref[0]


def _rwkv_post(y, r, k, v, g, ln_g, ln_b, r_k):
    bsz, t_len, _ = y.shape
    tb = min(256, t_len)
    spec = pl.BlockSpec((1, tb, GROUP_W), lambda b, i: (b, i, 0))
    par = pl.BlockSpec((1, GROUP_W), lambda b, i: (0, 0))
    row = lambda z: z.reshape(1, GROUP_W)
    return pl.pallas_call(
        _rwkv_post_kernel,
        out_shape=jax.ShapeDtypeStruct((bsz, t_len, GROUP_W), F32),
        grid=(bsz, t_len // tb),
        in_specs=[spec] * 5 + [par, par, par, pl.BlockSpec((3 * GROUP_W, GROUP_W), lambda b, i: (0, 0))],
        out_specs=spec,
        compiler_params=_cparams(("parallel", "parallel")),
        name="rwkv_post",
    )(y, r, k, v, g, row(ln_g), row(ln_b), row(r_k), _head_segment_ones())


DSA_KC = 512
NEG_BIG = -1e30


def _split_bf16(x):
    hi = x.astype(BF16)
    lo = (x - hi.astype(F32)).astype(BF16)
    return hi, lo


def _dsa_kernel(q_ref, qi_ref, kv_ref, o_ref, key_scr, bias_scr, *, topk):
    n = pl.program_id(1)
    t_len = kv_ref.shape[1]
    nq = Q_BLOCK
    kc = min(DSA_KC, t_len)
    n_chunks = (n * nq + nq + kc - 1) // kc
    t_lane = n * nq + lax.broadcasted_iota(I32, (1, nq), 1)
    limit = (t_lane // CHUNK + 1) * CHUNK

    qi = qi_ref[0]
    qi_rows = jnp.concatenate([qi[:, h * IDX_DIM:(h + 1) * IDX_DIM] for h in range(IDX_HEADS)], axis=0)
    qi_hi, qi_lo = _split_bf16(qi_rows)
    qi3 = jnp.concatenate([qi_hi, qi_hi, qi_lo], axis=-1)
    wi_pad = kv_ref[0, pl.ds(pl.multiple_of(n * nq, nq), nq), 3 * HEAD_DIM:4 * HEAD_DIM]
    eye_w = (lax.broadcasted_iota(I32, (SUBLANES, HEAD_DIM), 0)
             == lax.broadcasted_iota(I32, (SUBLANES, HEAD_DIM), 1)).astype(F32)
    wi_t = _dot_nt(eye_w, wi_pad, precision=HIGHEST)
    wi_t = wi_t * (IDX_HEADS ** -0.5 * IDX_DIM ** -0.5)

    def score_chunk(c, carry):
        k0 = pl.multiple_of(c * kc, kc)
        ki_hi, ki_lo = _split_bf16(kv_ref[0, pl.ds(k0, kc), 2 * HEAD_DIM:3 * HEAD_DIM])
        lg = _dot_nt(jnp.concatenate([ki_hi, ki_lo, ki_hi], axis=-1), qi3)
        score = None
        for h in range(IDX_HEADS):
            term = jnp.maximum(lg[:, h * nq:(h + 1) * nq], 0.0) * wi_t[h:h + 1, :]
            score = term if score is None else score + term
        score = jnp.where(score == 0.0, 0.0, score)
        bits = pltpu.bitcast(score, I32)
        okey = jnp.where(bits >= 0, bits, bits ^ jnp.int32(0x7FFFFFFF))
        pos = k0 + lax.broadcasted_iota(I32, (kc, nq), 0)
        key_scr[c] = jnp.where(pos < limit, okey, jnp.int32(INT_MIN))
        return carry

    lax.fori_loop(0, n_chunks, score_chunk, 0)

    acc_rows = 8 * SUBLANES

    def count_ge(cand):
        cand_rows = jnp.concatenate([cand] * (acc_rows // SUBLANES), axis=0)

        def body(c, acc):
            for s in range(kc // acc_rows):
                acc = acc + jnp.where(key_scr[c, s * acc_rows:(s + 1) * acc_rows, :] >= cand_rows, 1.0, 0.0)
            return acc
        acc = lax.fori_loop(0, n_chunks, body, jnp.zeros((acc_rows, nq), F32))
        return jnp.broadcast_to(jnp.sum(acc, axis=0, keepdims=True), (SUBLANES, nq))

    kf = jnp.float32(topk)
    cnt0 = count_ge(jnp.zeros((SUBLANES, nq), I32))
    m0 = jnp.where(cnt0 >= kf, jnp.int32(0), jnp.int32(INT_MIN))
    c0 = jnp.where(cnt0 >= kf, cnt0, kf)

    def bit_step(i, carry):
        m, cnt_m = carry
        cand = m | (jnp.int32(1) << (jnp.int32(30) - i))
        cnt = count_ge(cand)
        take = cnt >= kf
        return jnp.where(take, cand, m), jnp.where(take, cnt, cnt_m)

    thr8, cnt_thr = lax.fori_loop(0, 31, bit_step, (m0, c0))
    thr = thr8[0:1, :]
    n_tied = jnp.sum(jnp.where(cnt_thr == kf, 0.0, 1.0))

    def select_all_ties():
        def body(c, carry):
            keys = key_scr[c]
            bias_scr[c] = jnp.where((keys >= thr) & (keys != jnp.int32(INT_MIN)), 0.0, -jnp.inf)
            return carry
        lax.fori_loop(0, n_chunks, body, 0)

    def select_ranked_ties():
        need = kf - count_ge(thr8 + 1)[0:1, :]
        tri = (lax.broadcasted_iota(I32, (kc, kc), 1)
               <= lax.broadcasted_iota(I32, (kc, kc), 0)).astype(BF16)

        def body(c, running):
            keys = key_scr[c]
            tie = keys == thr
            prefix = _dot(tri, jnp.where(tie, 1.0, 0.0).astype(BF16))
            rank = running + prefix
            sel = (keys > thr) | (tie & (rank <= need) & (keys != jnp.int32(INT_MIN)))
            bias_scr[c] = jnp.where(sel, 0.0, -jnp.inf)
            return running + prefix[kc - 1:kc, :]
        lax.fori_loop(0, n_chunks, body, jnp.zeros((1, nq), F32))

    lax.cond(n_tied == 0.0, select_all_ties, select_ranked_ties)

    q = q_ref[0] * (HEAD_DIM ** -0.5)
    q_rows = jnp.concatenate([q[:, h * HEAD_DIM:(h + 1) * HEAD_DIM] for h in range(N_HEADS)],
                             axis=0).astype(BF16)
    cols = N_HEADS * nq
    head = lax.broadcasted_iota(I32, (1, cols), 1) // nq
    slope = pltpu.bitcast((126 - head) << 23, F32)
    eye_d = (lax.broadcasted_iota(I32, (HEAD_DIM, HEAD_DIM), 0)
             == lax.broadcasted_iota(I32, (HEAD_DIM, HEAD_DIM), 1)).astype(BF16)

    def attn_chunk(c, carry):
        m, l, acc = carry
        k0 = pl.multiple_of(c * kc, kc)
        kk = kv_ref[0, pl.ds(k0, kc), 0:HEAD_DIM].astype(BF16)
        vv = kv_ref[0, pl.ds(k0, kc), HEAD_DIM:2 * HEAD_DIM].astype(BF16)
        s = _dot_nt(kk, q_rows)
        pos = k0 + lax.broadcasted_iota(I32, (kc, nq), 0)
        dist = jnp.abs(t_lane - pos).astype(F32)
        s = (s - slope * jnp.concatenate([dist] * N_HEADS, axis=-1)
             + jnp.concatenate([bias_scr[c]] * N_HEADS, axis=-1))
        m_new = jnp.maximum(m, jnp.max(s, axis=0, keepdims=True))
        alpha = jnp.exp(m - m_new)
        p = jnp.exp(s - m_new)
        l = alpha * l + jnp.sum(p, axis=0, keepdims=True)
        v_t = _dot_nt(eye_d, vv).astype(BF16)
        acc = alpha * acc + _dot(v_t, p.astype(BF16))
        return m_new, l, acc

    _, l, acc = lax.fori_loop(0, n_chunks, attn_chunk,
                              (jnp.full((1, cols), NEG_BIG, F32), jnp.zeros((1, cols), F32),
                               jnp.zeros((HEAD_DIM, cols), F32)))
    out_t = acc / l
    eye_q = (lax.broadcasted_iota(I32, (nq, nq), 0)
             == lax.broadcasted_iota(I32, (nq, nq), 1)).astype(F32)
    o_ref[0] = jnp.concatenate(
        [_dot_nt(eye_q, out_t[:, h * nq:(h + 1) * nq], precision=HIGHEST) for h in range(N_HEADS)], axis=-1)


def _dsa_mix(proj):
    bsz, t_len, _ = proj.shape
    topk = min(TOPK_MAX, t_len // 4)
    kc = min(DSA_KC, t_len)
    nck = t_len // kc
    return pl.pallas_call(
        functools.partial(_dsa_kernel, topk=topk),
        out_shape=jax.ShapeDtypeStruct((bsz, t_len, GROUP_W), F32),
        grid=(bsz, t_len // Q_BLOCK),
        in_specs=[
            pl.BlockSpec((1, Q_BLOCK, 512), lambda b, n: (b, n, PB_Q // 512)),
            pl.BlockSpec((1, Q_BLOCK, 256), lambda b, n: (b, n, PB_QI // 256)),
            pl.BlockSpec((1, t_len, 256), lambda b, n: (b, 0, PB_KV // 256)),
        ],
        out_specs=pl.BlockSpec((1, Q_BLOCK, GROUP_W), lambda b, n: (b, n, 0)),
        scratch_shapes=[pltpu.VMEM((nck, kc, Q_BLOCK), I32),
                        pltpu.VMEM((nck, kc, Q_BLOCK), F32)],
        compiler_params=_cparams(("parallel", "arbitrary")),
        name="dsa_mix",
    )(proj, proj, proj)


BAND_TQ = LEFT_CHUNKS * CHUNK
BAND_W = (LEFT_CHUNKS + 1) * CHUNK


def _band_kernel(q_ref, kp_ref, kc_ref, vp_ref, vc_ref, bias_ref, o_ref):
    i = pl.program_id(1)
    tq = q_ref.shape[1]
    kwin = jnp.concatenate([kp_ref[0], kc_ref[0]], axis=0).astype(BF16)
    vwin = jnp.concatenate([vp_ref[0], vc_ref[0]], axis=0).astype(BF16)
    col = lax.broadcasted_iota(I32, (CHUNK, BAND_W), 1)
    for c in range(tq // CHUNK):
        r0 = c * CHUNK
        qc = q_ref[0, r0:r0 + CHUNK, :].astype(BF16)
        kc_ = kwin[r0:r0 + BAND_W]
        vc_ = vwin[r0:r0 + BAND_W]
        valid = jnp.logical_or(i > 0, col >= tq - r0)
        heads = [slice(h * HEAD_DIM, (h + 1) * HEAD_DIM) for h in range(N_HEADS)]
        s = jnp.concatenate([_dot_nt(qc[:, sl], kc_[:, sl]) for sl in heads], axis=0)
        s = s * (HEAD_DIM ** -0.5) + bias_ref[...]
        s = jnp.where(jnp.concatenate([valid] * N_HEADS, axis=0), s, -jnp.inf)
        mx = jnp.max(s, axis=-1, keepdims=True)
        p = jnp.exp(s - mx)
        inv_l = 1.0 / jnp.sum(p, axis=-1, keepdims=True)
        p = p.astype(BF16)
        outs = [_dot(p[h * CHUNK:(h + 1) * CHUNK], vc_[:, sl]) * inv_l[h * CHUNK:(h + 1) * CHUNK]
                for h, sl in enumerate(heads)]
        o_ref[0, r0:r0 + CHUNK, :] = jnp.concatenate(outs, axis=-1)


def _band_mix(proj, rel_bias):
    bsz, t_len, _ = proj.shape
    tq = BAND_TQ
    e = (BAND_W - 1) - np.arange(BAND_W + CHUNK - 1)
    seq = rel_bias[:, np.clip(e, -REL_CLIP, REL_CLIP) + REL_CLIP].astype(F32)
    bias = jnp.stack([seq[:, CHUNK - 1 - i:CHUNK - 1 - i + BAND_W] for i in range(CHUNK)], axis=1)
    bias = bias.reshape(N_HEADS * CHUNK, BAND_W)
    cur = lambda blk: pl.BlockSpec((1, tq, 512), lambda b, i: (b, i, blk))
    prev = lambda blk: pl.BlockSpec((1, tq, 512), lambda b, i: (b, jnp.maximum(i - 1, 0), blk))
    return pl.pallas_call(
        _band_kernel,
        out_shape=jax.ShapeDtypeStruct((bsz, t_len, GROUP_W), F32),
        grid=(bsz, t_len // tq),
        in_specs=[cur(PC_Q // 512), prev(PC_K // 512), cur(PC_K // 512),
                  prev(PC_V // 512), cur(PC_V // 512),
                  pl.BlockSpec((N_HEADS * CHUNK, BAND_W), lambda b, i: (0, 0))],
        out_specs=pl.BlockSpec((1, tq, GROUP_W), lambda b, i: (b, i, 0)),
        compiler_params=_cparams(("parallel", "parallel")),
        name="band_attn",
    )(proj, proj, proj, proj, proj, bias)


RET_TB = 512


def _ret_kernel(q_ref, k_ref, v_ref, g_ref, intra_ref, kvd_ref, qd_ref, cd_ref, o_ref, st_scr):
    @pl.when(pl.program_id(1) == 0)
    def _():
        st_scr[...] = jnp.zeros_like(st_scr)

    tb = q_ref.shape[1]
    eye = (lax.broadcasted_iota(I32, (D_KEY_DIM, D_KEY_DIM), 0)
           == lax.broadcasted_iota(I32, (D_KEY_DIM, D_KEY_DIM), 1)).astype(BF16)
    for c in range(tb // CHUNK):
        r0 = c * CHUNK
        q = q_ref[0, r0:r0 + CHUNK, :]
        k = k_ref[0, r0:r0 + CHUNK, :] * (D_KEY_DIM ** -0.5)
        v = v_ref[0, r0:r0 + CHUNK, :]
        g = g_ref[0, r0:r0 + CHUNK, :]
        outs = []
        for h in range(N_HEADS):
            ks = slice(h * D_KEY_DIM, (h + 1) * D_KEY_DIM)
            vs = slice(h * HEAD_DIM, (h + 1) * HEAD_DIM)
            qh, kh, vh = q[:, ks], k[:, ks], v[:, vs]
            vb = vh.astype(BF16)
            s = _dot_nt(qh.astype(BF16), kh.astype(BF16)) * intra_ref[h]
            intra = _dot(s.astype(BF16), vb)
            state = st_scr[h]
            inter = _dot((qh * qd_ref[h]).astype(BF16), state.astype(BF16))
            kd_t = _dot_nt(eye, (kh * kvd_ref[h]).astype(BF16)).astype(BF16)
            st_scr[h] = state * cd_ref[h] + _dot(kd_t, vb)
            y = intra + inter
            mu = jnp.mean(y, axis=-1, keepdims=True)
            d = y - mu
            var = jnp.mean(d * d, axis=-1, keepdims=True)
            outs.append(d * lax.rsqrt(var + RET_GN_EPS))
        o_ref[0, r0:r0 + CHUNK, :] = jnp.concatenate(outs, axis=-1) * _silu(g)


def _ret_mix(proj):
    bsz, t_len, _ = proj.shape
    tb = min(RET_TB, t_len)
    log_gamma = jnp.log1p(-(2.0 ** (-5.0 - jnp.arange(N_HEADS, dtype=F32))))
    pos = jnp.arange(CHUNK, dtype=F32)
    diff = pos[:, None] - pos[None, :]
    intra_decay = jnp.where(diff >= 0, jnp.exp(log_gamma[:, None, None] * jnp.maximum(diff, 0.0)), 0.0)
    kv_decay = jnp.exp(log_gamma[:, None] * (CHUNK - 1 - pos)[None, :])[:, :, None]
    q_decay = jnp.exp(log_gamma[:, None] * (pos + 1)[None, :])[:, :, None]
    chunk_decay = jnp.broadcast_to(jnp.exp(log_gamma * CHUNK)[:, None, None], (N_HEADS, 1, HEAD_DIM))
    full = lambda shp: pl.BlockSpec(shp, lambda b, i: (0,) * len(shp))
    return pl.pallas_call(
        _ret_kernel,
        out_shape=jax.ShapeDtypeStruct((bsz, t_len, GROUP_W), F32),
        grid=(bsz, t_len // tb),
        in_specs=[
            pl.BlockSpec((1, tb, 256), lambda b, i: (b, i, PD_Q // 256)),
            pl.BlockSpec((1, tb, 256), lambda b, i: (b, i, PD_K // 256)),
            pl.BlockSpec((1, tb, 512), lambda b, i: (b, i, PD_V // 512)),
            pl.BlockSpec((1, tb, 512), lambda b, i: (b, i, PD_G // 512)),
            full((N_HEADS, CHUNK, CHUNK)), full((N_HEADS, CHUNK, 1)), full((N_HEADS, CHUNK, 1)),
            full((N_HEADS, 1, HEAD_DIM)),
        ],
        out_specs=pl.BlockSpec((1, tb, GROUP_W), lambda b, i: (b, i, 0)),
        scratch_shapes=[pltpu.VMEM((N_HEADS, D_KEY_DIM, HEAD_DIM), F32)],
        compiler_params=_cparams(("parallel", "arbitrary")),
        name="retention",
    )(proj, proj, proj, proj, intra_decay, kv_decay, q_decay, chunk_decay)


def _deepnorm_ln(x, gate, branch, g, b):
    z = DEEPNORM_ALPHA * x + (1.0 + gate) * branch
    mu = jnp.mean(z, axis=-1, keepdims=True)
    d = z - mu
    var = jnp.mean(d * d, axis=-1, keepdims=True)
    return d * lax.rsqrt(var + LN_EPS) * g + b


def _outproj_kernel(ya_ref, yb_ref, yc_ref, yd_ref, w_ref, x_ref, gt_ref, g_ref, b_ref, o_ref):
    acc = None
    for gi, y_ref in enumerate((ya_ref, yb_ref, yc_ref, yd_ref)):
        part = _dot(y_ref[0].astype(BF16), w_ref[gi * GROUP_W:(gi + 1) * GROUP_W, :])
        acc = part if acc is None else acc + part
    o_ref[0] = _deepnorm_ln(x_ref[0], gt_ref[0], acc, g_ref[...], b_ref[...])


def _out_projection(ys, w, x, gt, ln_g, ln_b):
    bsz, t_len, d = x.shape
    tm = min(256, t_len)
    yspec = pl.BlockSpec((1, tm, GROUP_W), lambda b, i: (b, i, 0))
    xspec = pl.BlockSpec((1, tm, d), lambda b, i: (b, i, 0))
    vec = pl.BlockSpec((1, d), lambda b, i: (0, 0))
    return pl.pallas_call(
        _outproj_kernel,
        out_shape=jax.ShapeDtypeStruct(x.shape, F32),
        grid=(bsz, t_len // tm),
        in_specs=[yspec] * 4 + [pl.BlockSpec(w.shape, lambda b, i: (0, 0), pipeline_mode=pl.Buffered(1)),
                                xspec,
                                pl.BlockSpec((1, 1, d), lambda b, i: (b, 0, 0)), vec, vec],
        out_specs=xspec,
        compiler_params=_cparams(("parallel", "parallel")),
        name="out_proj_ln",
    )(*ys, w, x, gt, ln_g.reshape(1, d), ln_b.reshape(1, d))


def _up_kernel(x_ref, sc_ref, sh_ref, wg_ref, wu_ref, cw_ref, cb_ref, o_ref, h_scr, halo_scr):
    i = pl.program_id(1)
    j = pl.program_id(2)

    @pl.when(j == 0)
    def _():
        h_scr[...] = (x_ref[0] * (1.0 + sc_ref[0]) + sh_ref[0]).astype(BF16)

    @pl.when(i == 0)
    def _():
        halo_scr[j] = jnp.zeros(halo_scr.shape[1:], F32)

    h = h_scr[...]
    tm = h.shape[0]
    rows8 = lax.broadcasted_iota(I32, (SUBLANES, MXU_WIDTH), 0)
    for s in range(wg_ref.shape[1] // MXU_WIDTH):
        cols = slice(s * MXU_WIDTH, (s + 1) * MXU_WIDTH)
        gate = _dot(h, wg_ref[:, cols])
        up = _dot(h, wu_ref[:, cols])
        halo = halo_scr[j, :, cols]
        halo_scr[j, :, cols] = gate[tm - SUBLANES:, :]
        r1 = pltpu.roll(gate, 1, 0)
        r2 = pltpu.roll(gate, 2, 0)
        top1 = jnp.where(rows8 == 0, halo[SUBLANES - 1:SUBLANES, :], r1[:SUBLANES])
        top2 = jnp.where(rows8 == 0, halo[SUBLANES - 2:SUBLANES - 1, :],
                         jnp.where(rows8 == 1, halo[SUBLANES - 1:SUBLANES, :], r2[:SUBLANES]))
        g1 = jnp.concatenate([top1, r1[SUBLANES:]], axis=0)
        g2 = jnp.concatenate([top2, r2[SUBLANES:]], axis=0)
        cw = cw_ref[:, cols]
        conv = cw[0:1, :] * g2 + cw[1:2, :] * g1 + cw[2:3, :] * gate + cb_ref[:, cols]
        o_ref[0, :, cols] = (_silu(conv) * up).astype(BF16)


def _up_projection(x, sc, sh, wg, wu, conv_w, conv_b):
    bsz, t_len, d = x.shape
    ff = wg.shape[1]
    tm = min(1024, t_len)
    tn = 512
    return pl.pallas_call(
        _up_kernel,
        out_shape=jax.ShapeDtypeStruct((bsz, t_len, ff), BF16),
        grid=(bsz, t_len // tm, ff // tn),
        in_specs=[
            pl.BlockSpec((1, tm, d), lambda b, i, j: (b, i, 0)),
            pl.BlockSpec((1, 1, d), lambda b, i, j: (b, 0, 0)),
            pl.BlockSpec((1, 1, d), lambda b, i, j: (b, 0, 0)),
            pl.BlockSpec((d, tn), lambda b, i, j: (0, j)),
            pl.BlockSpec((d, tn), lambda b, i, j: (0, j)),
            pl.BlockSpec((CONV_W, tn), lambda b, i, j: (0, j)),
            pl.BlockSpec((1, tn), lambda b, i, j: (0, j)),
        ],
        out_specs=pl.BlockSpec((1, tm, tn), lambda b, i, j: (b, i, j)),
        scratch_shapes=[pltpu.VMEM((tm, d), BF16), pltpu.VMEM((ff // tn, SUBLANES, tn), F32)],
        compiler_params=_cparams(("parallel", "arbitrary", "arbitrary")),
        name="mlp_up_conv",
    )(x, sc, sh, wg, wu, conv_w, conv_b.reshape(1, ff))


def _down_kernel(f_ref, w_ref, x_ref, gt_ref, g_ref, b_ref, o_ref):
    acc = _dot(f_ref[0], w_ref[...])
    o_ref[0] = _deepnorm_ln(x_ref[0], gt_ref[0], acc, g_ref[...], b_ref[...])


def _down_projection(f, w, x, gt, ln_g, ln_b):
    bsz, t_len, d = x.shape
    ff = f.shape[2]
    tm = min(256, t_len)
    xspec = pl.BlockSpec((1, tm, d), lambda b, i: (b, i, 0))
    vec = pl.BlockSpec((1, d), lambda b, i: (0, 0))
    return pl.pallas_call(
        _down_kernel,
        out_shape=jax.ShapeDtypeStruct(x.shape, F32),
        grid=(bsz, t_len // tm),
        in_specs=[
            pl.BlockSpec((1, tm, ff), lambda b, i: (b, i, 0)),
            pl.BlockSpec((ff, d), lambda b, i: (0, 0), pipeline_mode=pl.Buffered(1)),
            xspec,
            pl.BlockSpec((1, 1, d), lambda b, i: (b, 0, 0)),
            vec, vec,
        ],
        out_specs=xspec,
        compiler_params=_cparams(("parallel", "parallel")),
        name="mlp_down_ln",
    )(f, w, x, gt, ln_g.reshape(1, d), ln_b.reshape(1, d))


_SRC_B = A_COLS
_SRC_C = A_COLS + B_COLS
_SRC_D = A_COLS + B_COLS + C_COLS
PACK_SEGMENTS = (
    (PC_Q, _SRC_C, C_COLS),
    (PD_Q, _SRC_D, D_COLS),
    (PB_Q, _SRC_B, GROUP_W),
    (PB_QI, _SRC_B + GROUP_W + 2 * HEAD_DIM, IDX_HEADS * IDX_DIM),
    (PB_KV, _SRC_B + GROUP_W, 2 * HEAD_DIM),
    (PB_KV + 2 * HEAD_DIM, _SRC_B + GROUP_W + 2 * HEAD_DIM + IDX_HEADS * IDX_DIM, IDX_DIM + IDX_HEADS),
    (PA, 0, A_COLS),
)
PACK_ZERO = ((PB_KV + 3 * HEAD_DIM + IDX_HEADS, PA), (PA + A_COLS, P_COLS))


def _pack_kernel(w_ref, o_ref):
    w = w_ref[0]
    for dst, src, width in PACK_SEGMENTS:
        o_ref[0, :, dst:dst + width] = w[:, src:src + width].astype(BF16)
    for lo, hi in PACK_ZERO:
        o_ref[0, :, lo:hi] = jnp.zeros((w.shape[0], hi - lo), BF16)


def _pack_w_in(w_in):
    depth, d, n_in = w_in.shape
    tk = 256
    return pl.pallas_call(
        _pack_kernel,
        out_shape=jax.ShapeDtypeStruct((depth, d, P_COLS), BF16),
        grid=(depth, d // tk),
        in_specs=[pl.BlockSpec((1, tk, n_in), lambda l, i: (l, i, 0))],
        out_specs=pl.BlockSpec((1, tk, P_COLS), lambda l, i: (l, i, 0)),
        compiler_params=_cparams(("parallel", "parallel")),
        name="pack_w_in",
    )(w_in)


def kernel(x, c, w_mod, b_mod, w_in, rwkv_mu, rwkv_w0, rwkv_w2, rwkv_a0, rwkv_a2, rwkv_g2,
           rwkv_kk, rwkv_ka, rwkv_rk, rwkv_ln_g, rwkv_ln_b, chunk_rel_bias, w_out, ln1_g, ln1_b,
           w_up, conv_w, conv_b, w_down, ln2_g, ln2_b):
    depth = w_mod.shape[0]
    d = x.shape[-1]
    mod = _modulation(c, w_mod, b_mod)
    w_in_packed = _pack_w_in(w_in)
    for l in range(depth):
        sh1, sc1, gt1, sh2, sc2, gt2 = [mod[l, :, None, i * d:(i + 1) * d] for i in range(6)]
        proj = _in_projection(x, sc1, sh1, w_in_packed[l])
        r, w, k, v, a, b, g = _rwkv_prep(proj, rwkv_mu[l], rwkv_w0[l], rwkv_w2[l], rwkv_a0[l],
                                         rwkv_a2[l], rwkv_g2[l], rwkv_kk[l], rwkv_ka[l])
        y = _rwkv_scan(r, w, k, v, a, b)
        y_a = _rwkv_post(y, r, k, v, g, rwkv_ln_g[l], rwkv_ln_b[l], rwkv_rk[l])
        y_b = _dsa_mix(proj)
        y_c = _band_mix(proj, chunk_rel_bias[l])
        y_d = _ret_mix(proj)
        x = _out_projection((y_a, y_b, y_c, y_d), w_out[l].astype(BF16), x, gt1, ln1_g[l], ln1_b[l])
        ff = w_down.shape[1]
        f = _up_projection(x, sc2, sh2, w_up[l, :, :ff].astype(BF16), w_up[l, :, ff:].astype(BF16),
                           conv_w[l], conv_b[l])
        x = _down_projection(f, w_down[l].astype(BF16), x, gt2, ln2_g[l], ln2_b[l])
    return x
```

```python
import functools

import numpy as np
import jax
import jax.numpy as jnp
from jax import lax
from jax.experimental import pallas as pl
from jax.experimental.pallas import tpu as pltpu

F32 = jnp.float32
BF16 = jnp.bfloat16
I32 = jnp.int32
HIGHEST = lax.Precision.HIGHEST

D_MODEL = 2048
DEPTH = 4
CHUNK = 64
GROUP_W = 512
HEAD_DIM = 64
N_HEADS = 8
A_LORA_W, A_LORA_A, A_LORA_G = 64, 64, 128
A_GN_EPS = HEAD_DIM * 1e-5
IDX_HEADS, IDX_DIM = 4, 64
TOPK_MAX = 256
Q_BLOCK = 128
LEFT_CHUNKS = 8
REL_CLIP = 256
D_KEY_DIM = 32
RET_GN_EPS = 1e-5
D_FF = 5632
CONV_W = 3
LN_EPS = 1e-5
DEEPNORM_ALPHA = (2 * DEPTH) ** 0.25

A_COLS = 3 * GROUP_W + A_LORA_W + A_LORA_A + A_LORA_G
B_COLS = GROUP_W + 2 * HEAD_DIM + IDX_HEADS * IDX_DIM + IDX_DIM + IDX_HEADS
C_COLS = 3 * GROUP_W
D_COLS = 2 * N_HEADS * D_KEY_DIM + 2 * GROUP_W

LANES = 128
SUBLANES = 8
MXU_WIDTH = 256
VMEM_LIMIT = 56 * 1024 * 1024

P_COLS = 6144
PC_Q, PC_K, PC_V = 0, 512, 1024
PD_Q, PD_K, PD_V, PD_G = 1536, 1792, 2048, 2560
PB_Q, PB_QI, PB_KV = 3072, 3584, 3840
PA = 4096
INT_MIN = -2 ** 31


def _cparams(sem):
    return pltpu.CompilerParams(dimension_semantics=sem, vmem_limit_bytes=VMEM_LIMIT)


def _sigmoid(x):
    return 1.0 / (1.0 + jnp.exp(-x))


def _silu(x):
    return x * _sigmoid(x)


def _dot(a, b, **kw):
    return jnp.dot(a, b, preferred_element_type=F32, **kw)


def _dot_nt(a, b, **kw):
    return lax.dot_general(a, b, (((1,), (1,)), ((), ())), preferred_element_type=F32, **kw)


def _mod_kernel(c_ref, w_ref, b_ref, o_ref):
    ca = _silu(c_ref[...])
    o_ref[0] = _dot(ca, w_ref[0], precision=HIGHEST) + b_ref[0]


def _modulation(c, w_mod, b_mod):
    depth, d, n6 = w_mod.shape
    bsz = c.shape[0]
    tn = 1024
    return pl.pallas_call(
        _mod_kernel,
        out_shape=jax.ShapeDtypeStruct((depth, bsz, n6), F32),
        grid=(depth, n6 // tn),
        in_specs=[
            pl.BlockSpec((bsz, d), lambda l, j: (0, 0)),
            pl.BlockSpec((1, d, tn), lambda l, j: (l, 0, j)),
            pl.BlockSpec((1, 1, tn), lambda l, j: (l, 0, j)),
        ],
        out_specs=pl.BlockSpec((1, bsz, tn), lambda l, j: (l, 0, j)),
        compiler_params=_cparams(("parallel", "parallel")),
        name="adaln_mod",
    )(c, w_mod, b_mod.reshape(depth, 1, n6))


def _inproj_kernel(x_ref, sc_ref, sh_ref, w_ref, o_ref, h_scr):
    @pl.when(pl.program_id(2) == 0)
    def _():
        h_scr[...] = (x_ref[0] * (1.0 + sc_ref[0]) + sh_ref[0]).astype(BF16)

    o_ref[0] = _dot(h_scr[...], w_ref[...])


def _in_projection(x, sc, sh, w):
    bsz, t_len, d = x.shape
    n = w.shape[1]
    tm = min(1024, t_len)
    tn = 1024
    return pl.pallas_call(
        _inproj_kernel,
        out_shape=jax.ShapeDtypeStruct((bsz, t_len, n), F32),
        grid=(bsz, t_len // tm, n // tn),
        in_specs=[
            pl.BlockSpec((1, tm, d), lambda b, i, j: (b, i, 0)),
            pl.BlockSpec((1, 1, d), lambda b, i, j: (b, 0, 0)),
            pl.BlockSpec((1, 1, d), lambda b, i, j: (b, 0, 0)),
            pl.BlockSpec((d, tn), lambda b, i, j: (0, j)),
        ],
        out_specs=pl.BlockSpec((1, tm, tn), lambda b, i, j: (b, i, j)),
        scratch_shapes=[pltpu.VMEM((tm, d), BF16)],
        compiler_params=_cparams(("parallel", "parallel", "arbitrary")),
        name="in_proj",
    )(x, sc, sh, w)


def _head_segment_ones():
    seg = (np.arange(GROUP_W)[:, None] // HEAD_DIM == np.arange(GROUP_W)[None, :] // HEAD_DIM)
    return jnp.asarray(np.concatenate([seg] * 3, axis=0), BF16)


def _head_sum(x, seg3):
    hi = x.astype(BF16)
    r1 = x - hi.astype(F32)
    mid = r1.astype(BF16)
    lo = (r1 - mid.astype(F32)).astype(BF16)
    return _dot(jnp.concatenate([hi, mid, lo], axis=-1), seg3)


def _rwkv_prep_kernel(p_ref, prev_ref, mu_ref, w0_ref, w2_ref, a0_ref, a2_ref, g2_ref,
                      kk_ref, ka_ref, seg_ref,
                      r_out, w_out, k_out, v_out, a_out, b_out, g_out):
    i = pl.program_id(1)
    p = p_ref[0][:, :A_COLS]
    last_prev = prev_ref[0][SUBLANES - 1:SUBLANES, :A_COLS]
    last_prev = jnp.where(i == 0, 0.0, last_prev)
    rows = lax.broadcasted_iota(I32, p.shape, 0)
    shifted = jnp.where(rows == 0, last_prev, pltpu.roll(p, 1, 0))
    xs = p + (shifted - p) * mu_ref[...]
    r = xs[:, 0:512]
    k = xs[:, 512:1024]
    v = xs[:, 1024:1536]
    wl = xs[:, 1536:1600]
    al = xs[:, 1600:1664]
    gl = xs[:, 1664:1792]
    z = w0_ref[...] + _dot(jnp.tanh(wl), w2_ref[...], precision=HIGHEST)
    nz = -z
    softplus = jnp.maximum(nz, 0.0) + jnp.log1p(jnp.exp(-jnp.abs(nz)))
    w_log = -softplus - 0.5
    a = _sigmoid(a0_ref[...] + _dot(al, a2_ref[...], precision=HIGHEST))
    kk = k * kk_ref[...]
    ssq = _head_sum(kk * kk, seg_ref[...])
    kk = kk / jnp.maximum(jnp.sqrt(ssq), 1e-12)
    r_out[0] = r
    w_out[0] = jnp.exp(-jnp.exp(w_log))
    k_out[0] = k * (1.0 + (a - 1.0) * ka_ref[...])
    v_out[0] = v
    a_out[0] = -kk
    b_out[0] = kk * a
    g_out[0] = _dot(_sigmoid(gl), g2_ref[...], precision=HIGHEST)


def _rwkv_prep(proj, mu, w0, w2, a0, a2, g2, k_k, k_a):
    bsz, t_len, _ = proj.shape
    tb = min(256, t_len)
    a_blk = PA // 2048
    row = lambda z: z.reshape(1, -1)
    full = lambda shp: pl.BlockSpec(shp, lambda b, i: (0,) * len(shp))
    out = jax.ShapeDtypeStruct((bsz, t_len, GROUP_W), F32)
    out_spec = pl.BlockSpec((1, tb, GROUP_W), lambda b, i: (b, i, 0))
    return pl.pallas_call(
        _rwkv_prep_kernel,
        out_shape=[out] * 7,
        grid=(bsz, t_len // tb),
        in_specs=[
            pl.BlockSpec((1, tb, 2048), lambda b, i: (b, i, a_blk)),
            pl.BlockSpec((1, SUBLANES, 2048),
                         lambda b, i: (b, jnp.maximum(i * (tb // SUBLANES) - 1, 0), a_blk)),
            full((1, A_COLS)), full((1, GROUP_W)), full((A_LORA_W, GROUP_W)),
            full((1, GROUP_W)), full((A_LORA_A, GROUP_W)), full((A_LORA_G, GROUP_W)),
            full((1, GROUP_W)), full((1, GROUP_W)), full((3 * GROUP_W, GROUP_W)),
        ],
        out_specs=[out_spec] * 7,
        compiler_params=_cparams(("parallel", "parallel")),
        name="rwkv_prep",
    )(proj, proj, row(mu), row(w0), w2, row(a0), a2, g2, row(k_k), row(k_a), _head_segment_ones())


SCAN_CHAINS = LANES // 2
SCAN_KH = HEAD_DIM // 2
SCAN_TB = 64


def _fold_halves(x):
    return x + pltpu.roll(x, SCAN_CHAINS, 1)


def _rwkv_scan_kernel(r_ref, w_ref, k_ref, v_ref, a_ref, b_ref, y_ref, s_scr, bc_scr):
    @pl.when(pl.program_id(1) == 0)
    def _():
        s_scr[...] = jnp.zeros_like(s_scr)

    tb = r_ref.shape[1]

    def step(t, carry):
        for idx, ref in enumerate((a_ref, w_ref, b_ref, k_ref, r_ref)):
            x = ref[0, t]
            for kk in range(SCAN_KH):
                bc_scr[idx, kk] = jnp.broadcast_to(x[kk:kk + 1, :], (SUBLANES, LANES))
        for g in range(HEAD_DIM // SUBLANES):
            vv = v_ref[0, t, g * SUBLANES:(g + 1) * SUBLANES, :]
            sa = None
            for kk in range(SCAN_KH):
                term = s_scr[g, kk] * bc_scr[0, kk]
                sa = term if sa is None else sa + term
            sa = _fold_halves(sa)
            y = None
            for kk in range(SCAN_KH):
                s = s_scr[g, kk] * bc_scr[1, kk] + sa * bc_scr[2, kk] + vv * bc_scr[3, kk]
                s_scr[g, kk] = s
                term = s * bc_scr[4, kk]
                y = term if y is None else y + term
            y_ref[0, t, g * SUBLANES:(g + 1) * SUBLANES, :] = _fold_halves(y)
        return carry

    lax.fori_loop(0, tb, step, 0)


def _to_scan_layout(z, key_split):
    bsz, t_len, _ = z.shape
    pad = (-bsz) % (SCAN_CHAINS // N_HEADS)
    if pad:
        z = jnp.pad(z, ((0, pad), (0, 0), (0, 0)))
    g = z.shape[0] * N_HEADS // SCAN_CHAINS
    bg = SCAN_CHAINS // N_HEADS
    if key_split:
        z = z.reshape(g, bg, t_len, N_HEADS, 2, SCAN_KH)
        z = jnp.transpose(z, (0, 2, 5, 4, 1, 3))
        return z.reshape(g, t_len, SCAN_KH, LANES)
    z = z.reshape(g, bg, t_len, N_HEADS, HEAD_DIM)
    z = jnp.transpose(z, (0, 2, 4, 1, 3)).reshape(g, t_len, HEAD_DIM, SCAN_CHAINS)
    return jnp.concatenate([z, z], axis=-1)


def _from_scan_layout(y, bsz):
    g, t_len = y.shape[:2]
    bg = SCAN_CHAINS // N_HEADS
    y = y[..., :SCAN_CHAINS].reshape(g, t_len, HEAD_DIM, bg, N_HEADS)
    y = jnp.transpose(y, (0, 3, 1, 4, 2)).reshape(g * bg, t_len, GROUP_W)
    return y[:bsz]


def _rwkv_scan(r, w, k, v, a, b):
    bsz, t_len, _ = r.shape
    rs, ws, ks, as_, bs = (_to_scan_layout(z, True) for z in (r, w, k, a, b))
    vs = _to_scan_layout(v, False)
    g = rs.shape[0]
    tb = min(SCAN_TB, t_len)
    kspec = pl.BlockSpec((1, tb, SCAN_KH, LANES), lambda gi, i: (gi, i, 0, 0))
    vspec = pl.BlockSpec((1, tb, HEAD_DIM, LANES), lambda gi, i: (gi, i, 0, 0))
    y = pl.pallas_call(
        _rwkv_scan_kernel,
        out_shape=jax.ShapeDtypeStruct((g, t_len, HEAD_DIM, LANES), F32),
        grid=(g, t_len // tb),
        in_specs=[kspec, kspec, kspec, vspec, kspec, kspec],
        out_specs=vspec,
        scratch_shapes=[pltpu.VMEM((HEAD_DIM // SUBLANES, SCAN_KH, SUBLANES, LANES), F32),
                        pltpu.VMEM((5, SCAN_KH, SUBLANES, LANES), F32)],
        compiler_params=_cparams(("parallel", "arbitrary")),
        name="rwkv_scan",
    )(rs, ws, ks, vs, as_, bs)
    return _from_scan_layout(y, bsz)


def _rwkv_post_kernel(y_ref, r_ref, k_ref, v_ref, g_ref, lng_ref, lnb_ref, rk_ref, seg_ref, o_ref):
    seg = seg_ref[...]
    y = y_ref[0]
    mu = _head_sum(y, seg) * (1.0 / HEAD_DIM)
    d = y - mu
    var = _head_sum(d * d, seg) * (1.0 / HEAD_DIM)
    yn = d * lax.rsqrt(var + A_GN_EPS) * lng_ref[...] + lnb_ref[...]
    bonus = _head_sum(r_ref[0] * k_ref[0] * rk_ref[...], seg) * v_ref[0]
    o_ref[0] = (yn + bonus) * g_ref[0]


def _rwkv_post(y, r, k, v, g, ln_g, ln_b, r_k):
    bsz, t_len, _ = y.shape
    tb = min(256, t_len)
    spec = pl.BlockSpec((1, tb, GROUP_W), lambda b, i: (b, i, 0))
    par = pl.BlockSpec((1, GROUP_W), lambda b, i: (0, 0))
    row = lambda z: z.reshape(1, GROUP_W)
    return pl.pallas_call(
        _rwkv_post_kernel,
        out_shape=jax.ShapeDtypeStruct((bsz, t_len, GROUP_W), F32),
        grid=(bsz, t_len // tb),
        in_specs=[spec] * 5 + [par, par, par, pl.BlockSpec((3 * GROUP_W, GROUP_W), lambda b, i: (0, 0))],
        out_specs=spec,
        compiler_params=_cparams(("parallel", "parallel")),
        name="rwkv_post",
    )(y, r, k, v, g, row(ln_g), row(ln_b), row(r_k), _head_segment_ones())


DSA_KC = 512
NEG_BIG = -1e30


def _split_bf16(x):
    hi = x.astype(BF16)
    lo = (x - hi.astype(F32)).astype(BF16)
    return hi, lo


def _dsa_kernel(q_ref, qi_ref, kv_ref, o_ref, key_scr, bias_scr, *, topk):
    n = pl.program_id(1)
    t_len = kv_ref.shape[1]
    nq = Q_BLOCK
    kc = min(DSA_KC, t_len)
    n_chunks = (n * nq + nq + kc - 1) // kc
    t_lane = n * nq + lax.broadcasted_iota(I32, (1, nq), 1)
    limit = (t_lane // CHUNK + 1) * CHUNK

    qi = qi_ref[0]
    qi_rows = jnp.concatenate([qi[:, h * IDX_DIM:(h + 1) * IDX_DIM] for h in range(IDX_HEADS)], axis=0)
    qi_hi, qi_lo = _split_bf16(qi_rows)
    qi3 = jnp.concatenate([qi_hi, qi_hi, qi_lo], axis=-1)
    wi_pad = kv_ref[0, pl.ds(pl.multiple_of(n * nq, nq), nq), 3 * HEAD_DIM:4 * HEAD_DIM]
    eye_w = (lax.broadcasted_iota(I32, (SUBLANES, HEAD_DIM), 0)
             == lax.broadcasted_iota(I32, (SUBLANES, HEAD_DIM), 1)).astype(F32)
    wi_t = _dot_nt(eye_w, wi_pad, precision=HIGHEST)
    wi_t = wi_t * (IDX_HEADS ** -0.5 * IDX_DIM ** -0.5)

    def score_chunk(c, carry):
        k0 = pl.multiple_of(c * kc, kc)
        ki_hi, ki_lo = _split_bf16(kv_ref[0, pl.ds(k0, kc), 2 * HEAD_DIM:3 * HEAD_DIM])
        lg = _dot_nt(jnp.concatenate([ki_hi, ki_lo, ki_hi], axis=-1), qi3)
        score = None
        for h in range(IDX_HEADS):
            term = jnp.maximum(lg[:, h * nq:(h + 1) * nq], 0.0) * wi_t[h:h + 1, :]
            score = term if score is None else score + term
        score = jnp.where(score == 0.0, 0.0, score)
        bits = pltpu.bitcast(score, I32)
        okey = jnp.where(bits >= 0, bits, bits ^ jnp.int32(0x7FFFFFFF))
        pos = k0 + lax.broadcasted_iota(I32, (kc, nq), 0)
        key_scr[c] = jnp.where(pos < limit, okey, jnp.int32(INT_MIN))
        return carry

    lax.fori_loop(0, n_chunks, score_chunk, 0)

    acc_rows = 8 * SUBLANES

    def count_ge(cand):
        cand_rows = jnp.concatenate([cand] * (acc_rows // SUBLANES), axis=0)

        def body(c, acc):
            for s in range(kc // acc_rows):
                acc = acc + jnp.where(key_scr[c, s * acc_rows:(s + 1) * acc_rows, :] >= cand_rows, 1.0, 0.0)
            return acc
        acc = lax.fori_loop(0, n_chunks, body, jnp.zeros((acc_rows, nq), F32))
        return jnp.broadcast_to(jnp.sum(acc, axis=0, keepdims=True), (SUBLANES, nq))

    kf = jnp.float32(topk)
    cnt0 = count_ge(jnp.zeros((SUBLANES, nq), I32))
    m0 = jnp.where(cnt0 >= kf, jnp.int32(0), jnp.int32(INT_MIN))
    c0 = jnp.where(cnt0 >= kf, cnt0, kf)

    def bit_step(i, carry):
        m, cnt_m = carry
        cand = m | (jnp.int32(1) << (jnp.int32(30) - i))
        cnt = count_ge(cand)
        take = cnt >= kf
        return jnp.where(take, cand, m), jnp.where(take, cnt, cnt_m)

    thr8, cnt_thr = lax.fori_loop(0, 31, bit_step, (m0, c0))
    thr = thr8[0:1, :]
    n_tied = jnp.sum(jnp.where(cnt_thr == kf, 0.0, 1.0))

    def select_all_ties():
        def body(c, carry):
            keys = key_scr[c]
            bias_scr[c] = jnp.where((keys >= thr) & (keys != jnp.int32(INT_MIN)), 0.0, -jnp.inf)
            return carry
        lax.fori_loop(0, n_chunks, body, 0)

    def select_ranked_ties():
        need = kf - count_ge(thr8 + 1)[0:1, :]
        tri = (lax.broadcasted_iota(I32, (kc, kc), 1)
               <= lax.broadcasted_iota(I32, (kc, kc), 0)).astype(BF16)

        def body(c, running):
            keys = key_scr[c]
            tie = keys == thr
            prefix = _dot(tri, jnp.where(tie, 1.0, 0.0).astype(BF16))
            rank = running + prefix
            sel = (keys > thr) | (tie & (rank <= need) & (keys != jnp.int32(INT_MIN)))
            bias_scr[c] = jnp.where(sel, 0.0, -jnp.inf)
            return running + prefix[kc - 1:kc, :]
        lax.fori_loop(0, n_chunks, body, jnp.zeros((1, nq), F32))

    lax.cond(n_tied == 0.0, select_all_ties, select_ranked_ties)

    q = q_ref[0] * (HEAD_DIM ** -0.5)
    q_rows = jnp.concatenate([q[:, h * HEAD_DIM:(h + 1) * HEAD_DIM] for h in range(N_HEADS)],
                             axis=0).astype(BF16)
    cols = N_HEADS * nq
    head = lax.broadcasted_iota(I32, (1, cols), 1) // nq
    slope = pltpu.bitcast((126 - head) << 23, F32)
    eye_d = (lax.broadcasted_iota(I32, (HEAD_DIM, HEAD_DIM), 0)
             == lax.broadcasted_iota(I32, (HEAD_DIM, HEAD_DIM), 1)).astype(BF16)

    def attn_chunk(c, carry):
        m, l, acc = carry
        k0 = pl.multiple_of(c * kc, kc)
        kk = kv_ref[0, pl.ds(k0, kc), 0:HEAD_DIM].astype(BF16)
        vv = kv_ref[0, pl.ds(k0, kc), HEAD_DIM:2 * HEAD_DIM].astype(BF16)
        s = _dot_nt(kk, q_rows)
        pos = k0 + lax.broadcasted_iota(I32, (kc, nq), 0)
        dist = jnp.abs(t_lane - pos).astype(F32)
        s = (s - slope * jnp.concatenate([dist] * N_HEADS, axis=-1)
             + jnp.concatenate([bias_scr[c]] * N_HEADS, axis=-1))
        m_new = jnp.maximum(m, jnp.max(s, axis=0, keepdims=True))
        alpha = jnp.exp(m - m_new)
        p = jnp.exp(s - m_new)
        l = alpha * l + jnp.sum(p, axis=0, keepdims=True)
        v_t = _dot_nt(eye_d, vv).astype(BF16)
        acc = alpha * acc + _dot(v_t, p.astype(BF16))
        return m_new, l, acc

    _, l, acc = lax.fori_loop(0, n_chunks, attn_chunk,
                              (jnp.full((1, cols), NEG_BIG, F32), jnp.zeros((1, cols), F32),
                               jnp.zeros((HEAD_DIM, cols), F32)))
    out_t = acc / l
    eye_q = (lax.broadcasted_iota(I32, (nq, nq), 0)
             == lax.broadcasted_iota(I32, (nq, nq), 1)).astype(F32)
    o_ref[0] = jnp.concatenate(
        [_dot_nt(eye_q, out_t[:, h * nq:(h + 1) * nq], precision=HIGHEST) for h in range(N_HEADS)], axis=-1)


def _dsa_mix(proj):
    bsz, t_len, _ = proj.shape
    topk = min(TOPK_MAX, t_len // 4)
    kc = min(DSA_KC, t_len)
    nck = t_len // kc
    return pl.pallas_call(
        functools.partial(_dsa_kernel, topk=topk),
        out_shape=jax.ShapeDtypeStruct((bsz, t_len, GROUP_W), F32),
        grid=(bsz, t_len // Q_BLOCK),
        in_specs=[
            pl.BlockSpec((1, Q_BLOCK, 512), lambda b, n: (b, n, PB_Q // 512)),
            pl.BlockSpec((1, Q_BLOCK, 256), lambda b, n: (b, n, PB_QI // 256)),
            pl.BlockSpec((1, t_len, 256), lambda b, n: (b, 0, PB_KV // 256)),
        ],
        out_specs=pl.BlockSpec((1, Q_BLOCK, GROUP_W), lambda b, n: (b, n, 0)),
        scratch_shapes=[pltpu.VMEM((nck, kc, Q_BLOCK), I32),
                        pltpu.VMEM((nck, kc, Q_BLOCK), F32)],
        compiler_params=_cparams(("parallel", "arbitrary")),
        name="dsa_mix",
    )(proj, proj, proj)


BAND_TQ = LEFT_CHUNKS * CHUNK
BAND_W = (LEFT_CHUNKS + 1) * CHUNK


def _band_kernel(q_ref, kp_ref, kc_ref, vp_ref, vc_ref, bias_ref, o_ref):
    i = pl.program_id(1)
    tq = q_ref.shape[1]
    kwin = jnp.concatenate([kp_ref[0], kc_ref[0]], axis=0).astype(BF16)
    vwin = jnp.concatenate([vp_ref[0], vc_ref[0]], axis=0).astype(BF16)
    col = lax.broadcasted_iota(I32, (CHUNK, BAND_W), 1)
    for c in range(tq // CHUNK):
        r0 = c * CHUNK
        qc = q_ref[0, r0:r0 + CHUNK, :].astype(BF16)
        kc_ = kwin[r0:r0 + BAND_W]
        vc_ = vwin[r0:r0 + BAND_W]
        valid = jnp.logical_or(i > 0, col >= tq - r0)
        heads = [slice(h * HEAD_DIM, (h + 1) * HEAD_DIM) for h in range(N_HEADS)]
        s = jnp.concatenate([_dot_nt(qc[:, sl], kc_[:, sl]) for sl in heads], axis=0)
        s = s * (HEAD_DIM ** -0.5) + bias_ref[...]
        s = jnp.where(jnp.concatenate([valid] * N_HEADS, axis=0), s, -jnp.inf)
        mx = jnp.max(s, axis=-1, keepdims=True)
        p = jnp.exp(s - mx)
        inv_l = 1.0 / jnp.sum(p, axis=-1, keepdims=True)
        p = p.astype(BF16)
        outs = [_dot(p[h * CHUNK:(h + 1) * CHUNK], vc_[:, sl]) * inv_l[h * CHUNK:(h + 1) * CHUNK]
                for h, sl in enumerate(heads)]
        o_ref[0, r0:r0 + CHUNK, :] = jnp.concatenate(outs, axis=-1)


def _band_mix(proj, rel_bias):
    bsz, t_len, _ = proj.shape
    tq = BAND_TQ
    e = (BAND_W - 1) - np.arange(BAND_W + CHUNK - 1)
    seq = rel_bias[:, np.clip(e, -REL_CLIP, REL_CLIP) + REL_CLIP].astype(F32)
    bias = jnp.stack([seq[:, CHUNK - 1 - i:CHUNK - 1 - i + BAND_W] for i in range(CHUNK)], axis=1)
    bias = bias.reshape(N_HEADS * CHUNK, BAND_W)
    cur = lambda blk: pl.BlockSpec((1, tq, 512), lambda b, i: (b, i, blk))
    prev = lambda blk: pl.BlockSpec((1, tq, 512), lambda b, i: (b, jnp.maximum(i - 1, 0), blk))
    return pl.pallas_call(
        _band_kernel,
        out_shape=jax.ShapeDtypeStruct((bsz, t_len, GROUP_W), F32),
        grid=(bsz, t_len // tq),
        in_specs=[cur(PC_Q // 512), prev(PC_K // 512), cur(PC_K // 512),
                  prev(PC_V // 512), cur(PC_V // 512),
                  pl.BlockSpec((N_HEADS * CHUNK, BAND_W), lambda b, i: (0, 0))],
        out_specs=pl.BlockSpec((1, tq, GROUP_W), lambda b, i: (b, i, 0)),
        compiler_params=_cparams(("parallel", "parallel")),
        name="band_attn",
    )(proj, proj, proj, proj, proj, bias)


RET_TB = 512


def _ret_kernel(q_ref, k_ref, v_ref, g_ref, intra_ref, kvd_ref, qd_ref, cd_ref, o_ref, st_scr):
    @pl.when(pl.program_id(1) == 0)
    def _():
        st_scr[...] = jnp.zeros_like(st_scr)

    tb = q_ref.shape[1]
    eye = (lax.broadcasted_iota(I32, (D_KEY_DIM, D_KEY_DIM), 0)
           == lax.broadcasted_iota(I32, (D_KEY_DIM, D_KEY_DIM), 1)).astype(BF16)
    for c in range(tb // CHUNK):
        r0 = c * CHUNK
        q = q_ref[0, r0:r0 + CHUNK, :]
        k = k_ref[0, r0:r0 + CHUNK, :] * (D_KEY_DIM ** -0.5)
        v = v_ref[0, r0:r0 + CHUNK, :]
        g = g_ref[0, r0:r0 + CHUNK, :]
        outs = []
        for h in range(N_HEADS):
            ks = slice(h * D_KEY_DIM, (h + 1) * D_KEY_DIM)
            vs = slice(h * HEAD_DIM, (h + 1) * HEAD_DIM)
            qh, kh, vh = q[:, ks], k[:, ks], v[:, vs]
            vb = vh.astype(BF16)
            s = _dot_nt(qh.astype(BF16), kh.astype(BF16)) * intra_ref[h]
            intra = _dot(s.astype(BF16), vb)
            state = st_scr[h]
            inter = _dot((qh * qd_ref[h]).astype(BF16), state.astype(BF16))
            kd_t = _dot_nt(eye, (kh * kvd_ref[h]).astype(BF16)).astype(BF16)
            st_scr[h] = state * cd_ref[h] + _dot(kd_t, vb)
            y = intra + inter
            mu = jnp.mean(y, axis=-1, keepdims=True)
            d = y - mu
            var = jnp.mean(d * d, axis=-1, keepdims=True)
            outs.append(d * lax.rsqrt(var + RET_GN_EPS))
        o_ref[0, r0:r0 + CHUNK, :] = jnp.concatenate(outs, axis=-1) * _silu(g)


def _ret_mix(proj):
    bsz, t_len, _ = proj.shape
    tb = min(RET_TB, t_len)
    log_gamma = jnp.log1p(-(2.0 ** (-5.0 - jnp.arange(N_HEADS, dtype=F32))))
    pos = jnp.arange(CHUNK, dtype=F32)
    diff = pos[:, None] - pos[None, :]
    intra_decay = jnp.where(diff >= 0, jnp.exp(log_gamma[:, None, None] * jnp.maximum(diff, 0.0)), 0.0)
    kv_decay = jnp.exp(log_gamma[:, None] * (CHUNK - 1 - pos)[None, :])[:, :, None]
    q_decay = jnp.exp(log_gamma[:, None] * (pos + 1)[None, :])[:, :, None]
    chunk_decay = jnp.broadcast_to(jnp.exp(log_gamma * CHUNK)[:, None, None], (N_HEADS, 1, HEAD_DIM))
    full = lambda shp: pl.BlockSpec(shp, lambda b, i: (0,) * len(shp))
    return pl.pallas_call(
        _ret_kernel,
        out_shape=jax.ShapeDtypeStruct((bsz, t_len, GROUP_W), F32),
        grid=(bsz, t_len // tb),
        in_specs=[
            pl.BlockSpec((1, tb, 256), lambda b, i: (b, i, PD_Q // 256)),
            pl.BlockSpec((1, tb, 256), lambda b, i: (b, i, PD_K // 256)),
            pl.BlockSpec((1, tb, 512), lambda b, i: (b, i, PD_V // 512)),
            pl.BlockSpec((1, tb, 512), lambda b, i: (b, i, PD_G // 512)),
            full((N_HEADS, CHUNK, CHUNK)), full((N_HEADS, CHUNK, 1)), full((N_HEADS, CHUNK, 1)),
            full((N_HEADS, 1, HEAD_DIM)),
        ],
        out_specs=pl.BlockSpec((1, tb, GROUP_W), lambda b, i: (b, i, 0)),
        scratch_shapes=[pltpu.VMEM((N_HEADS, D_KEY_DIM, HEAD_DIM), F32)],
        compiler_params=_cparams(("parallel", "arbitrary")),
        name="retention",
    )(proj, proj, proj, proj, intra_decay, kv_decay, q_decay, chunk_decay)


def _deepnorm_ln(x, gate, branch, g, b):
    z = DEEPNORM_ALPHA * x + (1.0 + gate) * branch
    mu = jnp.mean(z, axis=-1, keepdims=True)
    d = z - mu
    var = jnp.mean(d * d, axis=-1, keepdims=True)
    return d * lax.rsqrt(var + LN_EPS) * g + b


def _outproj_kernel(ya_ref, yb_ref, yc_ref, yd_ref, w_ref, x_ref, gt_ref, g_ref, b_ref, o_ref):
    acc = None
    for gi, y_ref in enumerate((ya_ref, yb_ref, yc_ref, yd_ref)):
        part = _dot(y_ref[0].astype(BF16), w_ref[gi * GROUP_W:(gi + 1) * GROUP_W, :])
        acc = part if acc is None else acc + part
    o_ref[0] = _deepnorm_ln(x_ref[0], gt_ref[0], acc, g_ref[...], b_ref[...])


def _out_projection(ys, w, x, gt, ln_g, ln_b):
    bsz, t_len, d = x.shape
    tm = min(256, t_len)
    yspec = pl.BlockSpec((1, tm, GROUP_W), lambda b, i: (b, i, 0))
    xspec = pl.BlockSpec((1, tm, d), lambda b, i: (b, i, 0))
    vec = pl.BlockSpec((1, d), lambda b, i: (0, 0))
    return pl.pallas_call(
        _outproj_kernel,
        out_shape=jax.ShapeDtypeStruct(x.shape, F32),
        grid=(bsz, t_len // tm),
        in_specs=[yspec] * 4 + [pl.BlockSpec(w.shape, lambda b, i: (0, 0), pipeline_mode=pl.Buffered(1)),
                                xspec,
                                pl.BlockSpec((1, 1, d), lambda b, i: (b, 0, 0)), vec, vec],
        out_specs=xspec,
        compiler_params=_cparams(("parallel", "parallel")),
        name="out_proj_ln",
    )(*ys, w, x, gt, ln_g.reshape(1, d), ln_b.reshape(1, d))


def _up_kernel(x_ref, sc_ref, sh_ref, wg_ref, wu_ref, cw_ref, cb_ref, o_ref, h_scr, halo_scr):
    i = pl.program_id(1)
    j = pl.program_id(2)

    @pl.when(j == 0)
    def _():
        h_scr[...] = (x_ref[0] * (1.0 + sc_ref[0]) + sh_ref[0]).astype(BF16)

    @pl.when(i == 0)
    def _():
        halo_scr[j] = jnp.zeros(halo_scr.shape[1:], F32)

    h = h_scr[...]
    tm = h.shape[0]
    rows8 = lax.broadcasted_iota(I32, (SUBLANES, MXU_WIDTH), 0)
    for s in range(wg_ref.shape[1] // MXU_WIDTH):
        cols = slice(s * MXU_WIDTH, (s + 1) * MXU_WIDTH)
        gate = _dot(h, wg_ref[:, cols])
        up = _dot(h, wu_ref[:, cols])
        halo = halo_scr[j, :, cols]
        halo_scr[j, :, cols] = gate[tm - SUBLANES:, :]
        r1 = pltpu.roll(gate, 1, 0)
        r2 = pltpu.roll(gate, 2, 0)
        top1 = jnp.where(rows8 == 0, halo[SUBLANES - 1:SUBLANES, :], r1[:SUBLANES])
        top2 = jnp.where(rows8 == 0, halo[SUBLANES - 2:SUBLANES - 1, :],
                         jnp.where(rows8 == 1, halo[SUBLANES - 1:SUBLANES, :], r2[:SUBLANES]))
        g1 = jnp.concatenate([top1, r1[SUBLANES:]], axis=0)
        g2 = jnp.concatenate([top2, r2[SUBLANES:]], axis=0)
        cw = cw_ref[:, cols]
        conv = cw[0:1, :] * g2 + cw[1:2, :] * g1 + cw[2:3, :] * gate + cb_ref[:, cols]
        o_ref[0, :, cols] = (_silu(conv) * up).astype(BF16)


def _up_projection(x, sc, sh, wg, wu, conv_w, conv_b):
    bsz, t_len, d = x.shape
    ff = wg.shape[1]
    tm = min(1024, t_len)
    tn = 512
    return pl.pallas_call(
        _up_kernel,
        out_shape=jax.ShapeDtypeStruct((bsz, t_len, ff), BF16),
        grid=(bsz, t_len // tm, ff // tn),
        in_specs=[
            pl.BlockSpec((1, tm, d), lambda b, i, j: (b, i, 0)),
            pl.BlockSpec((1, 1, d), lambda b, i, j: (b, 0, 0)),
            pl.BlockSpec((1, 1, d), lambda b, i, j: (b, 0, 0)),
            pl.BlockSpec((d, tn), lambda b, i, j: (0, j)),
            pl.BlockSpec((d, tn), lambda b, i, j: (0, j)),
            pl.BlockSpec((CONV_W, tn), lambda b, i, j: (0, j)),
            pl.BlockSpec((1, tn), lambda b, i, j: (0, j)),
        ],
        out_specs=pl.BlockSpec((1, tm, tn), lambda b, i, j: (b, i, j)),
        scratch_shapes=[pltpu.VMEM((tm, d), BF16), pltpu.VMEM((ff // tn, SUBLANES, tn), F32)],
        compiler_params=_cparams(("parallel", "arbitrary", "arbitrary")),
        name="mlp_up_conv",
    )(x, sc, sh, wg, wu, conv_w, conv_b.reshape(1, ff))


def _down_kernel(f_ref, w_ref, x_ref, gt_ref, g_ref, b_ref, o_ref):
    acc = _dot(f_ref[0], w_ref[...])
    o_ref[0] = _deepnorm_ln(x_ref[0], gt_ref[0], acc, g_ref[...], b_ref[...])


def _down_projection(f, w, x, gt, ln_g, ln_b):
    bsz, t_len, d = x.shape
    ff = f.shape[2]
    tm = min(256, t_len)
    xspec = pl.BlockSpec((1, tm, d), lambda b, i: (b, i, 0))
    vec = pl.BlockSpec((1, d), lambda b, i: (0, 0))
    return pl.pallas_call(
        _down_kernel,
        out_shape=jax.ShapeDtypeStruct(x.shape, F32),
        grid=(bsz, t_len // tm),
        in_specs=[
            pl.BlockSpec((1, tm, ff), lambda b, i: (b, i, 0)),
            pl.BlockSpec((ff, d), lambda b, i: (0, 0), pipeline_mode=pl.Buffered(1)),
            xspec,
            pl.BlockSpec((1, 1, d), lambda b, i: (b, 0, 0)),
            vec, vec,
        ],
        out_specs=xspec,
        compiler_params=_cparams(("parallel", "parallel")),
        name="mlp_down_ln",
    )(f, w, x, gt, ln_g.reshape(1, d), ln_b.reshape(1, d))


_SRC_B = A_COLS
_SRC_C = A_COLS + B_COLS
_SRC_D = A_COLS + B_COLS + C_COLS
PACK_SEGMENTS = (
    (PC_Q, _SRC_C, C_COLS),
    (PD_Q, _SRC_D, D_COLS),
    (PB_Q, _SRC_B, GROUP_W),
    (PB_QI, _SRC_B + GROUP_W + 2 * HEAD_DIM, IDX_HEADS * IDX_DIM),
    (PB_KV, _SRC_B + GROUP_W, 2 * HEAD_DIM),
    (PB_KV + 2 * HEAD_DIM, _SRC_B + GROUP_W + 2 * HEAD_DIM + IDX_HEADS * IDX_DIM, IDX_DIM + IDX_HEADS),
    (PA, 0, A_COLS),
)
PACK_ZERO = ((PB_KV + 3 * HEAD_DIM + IDX_HEADS, PA), (PA + A_COLS, P_COLS))


def _pack_kernel(w_ref, o_ref):
    w = w_ref[0]
    for dst, src, width in PACK_SEGMENTS:
        o_ref[0, :, dst:dst + width] = w[:, src:src + width].astype(BF16)
    for lo, hi in PACK_ZERO:
        o_ref[0, :, lo:hi] = jnp.zeros((w.shape[0], hi - lo), BF16)


def _pack_w_in(w_in):
    depth, d, n_in = w_in.shape
    tk = 256
    return pl.pallas_call(
        _pack_kernel,
        out_shape=jax.ShapeDtypeStruct((depth, d, P_COLS), BF16),
        grid=(depth, d // tk),
        in_specs=[pl.BlockSpec((1, tk, n_in), lambda l, i: (l, i, 0))],
        out_specs=pl.BlockSpec((1, tk, P_COLS), lambda l, i: (l, i, 0)),
        compiler_params=_cparams(("parallel", "parallel")),
        name="pack_w_in",
    )(w_in)


def kernel(x, c, w_mod, b_mod, w_in, rwkv_mu, rwkv_w0, rwkv_w2, rwkv_a0, rwkv_a2, rwkv_g2,
           rwkv_kk, rwkv_ka, rwkv_rk, rwkv_ln_g, rwkv_ln_b, chunk_rel_bias, w_out, ln1_g, ln1_b,
           w_up, conv_w, conv_b, w_down, ln2_g, ln2_b):
    depth = w_mod.shape[0]
    d = x.shape[-1]
    mod = _modulation(c, w_mod, b_mod)
    w_in_packed = _pack_w_in(w_in)
    for l in range(depth):
        sh1, sc1, gt1, sh2, sc2, gt2 = [mod[l, :, None, i * d:(i + 1) * d] for i in range(6)]
        proj = _in_projection(x, sc1, sh1, w_in_packed[l])
        r, w, k, v, a, b, g = _rwkv_prep(proj, rwkv_mu[l], rwkv_w0[l], rwkv_w2[l], rwkv_a0[l],
                                         rwkv_a2[l], rwkv_g2[l], rwkv_kk[l], rwkv_ka[l])
        y = _rwkv_scan(r, w, k, v, a, b)
        y_a = _rwkv_post(y, r, k, v, g, rwkv_ln_g[l], rwkv_ln_b[l], rwkv_rk[l])
        y_b = _dsa_mix(proj)
        y_c = _band_mix(proj, chunk_rel_bias[l])
        y_d = _ret_mix(proj)
        x = _out_projection((y_a, y_b, y_c, y_d), w_out[l].astype(BF16), x, gt1, ln1_g[l], ln1_b[l])
        ff = w_down.shape[1]
        f = _up_projection(x, sc2, sh2, w_up[l, :, :ff].astype(BF16), w_up[l, :, ff:].astype(BF16),
                           conv_w[l], conv_b[l])
        x = _down_projection(f, w_down[l].astype(BF16), x, gt2, ln2_g[l], ln2_b[l])
    return x
```
